```python
import functools
import jax, jax.numpy as jnp
from jax import lax
import numpy as np

D_MODEL = 1024
BATCH = 8
SEQ = 2048
DEPTH = 4
DEC_BATCH = 128
DEC_SEQ = 1
PAST_LEN = 2048
PAGE_SIZE = 128

N_META = 16
BLOCK = 128
SB_HEADS = 8
SB_HEAD_DIM = 64
SB_WIDTH = SB_HEADS * SB_HEAD_DIM
SB_SCALE = SB_HEAD_DIM ** -0.5
SB_BIAS_INIT = -7.0
SSM_HEADS = 16
SSM_HEAD_DIM = 64
SSM_INNER = SSM_HEADS * SSM_HEAD_DIM
SSM_GROUPS = 4
SSM_HEADS_PER_GROUP = SSM_HEADS // SSM_GROUPS
SSM_STATE = 128
SSM_CONV = 4
CONV_DIM = SSM_INNER + 2 * SSM_GROUPS * SSM_STATE
FFN_DIM = 2816
FFN_CONV = 3
EPS = 1e-6
IN_SPLITS = (SB_WIDTH, 2 * SB_WIDTH, 3 * SB_WIDTH, 3 * SB_WIDTH + SSM_INNER,
             3 * SB_WIDTH + SSM_INNER + CONV_DIM, 3 * SB_WIDTH + SSM_INNER + CONV_DIM + SSM_HEADS)
IN_WIDTH = IN_SPLITS[-1] + 2 * D_MODEL

kernel_name = 'stickbreak_ssd_parallel_hybrid_step'


def rmsnorm(x, w):
    xf = x.astype(jnp.float32)
    y = xf * lax.rsqrt(jnp.mean(xf * xf, axis=-1, keepdims=True) + EPS)
    return (y * w.astype(jnp.float32)).astype(x.dtype)


def causal_dwconv(u, w, b):
    width = w.shape[0]
    n = u.shape[1] - width + 1
    out = b
    for j in range(width):
        out = out + u[:, j:j + n] * w[j]
    return out


def stick_breaking_block(q, k, v, q_pos, k_ok, bias):
    n_keys = k.shape[1]
    z = (jnp.einsum('bqhd,bshd->bhqs', q, k).astype(jnp.float32) * SB_SCALE
         + bias.astype(jnp.float32)[None, :, None, None])
    mask = ((jnp.arange(n_keys)[None, :] < q_pos[:, None]) & k_ok[None, :])[None, None]
    log_keep = jnp.where(mask, jax.nn.log_sigmoid(-z), 0.0)
    between = lax.cumsum(log_keep, axis=3, reverse=True) - log_keep
    w = jnp.where(mask, jnp.exp(jax.nn.log_sigmoid(z) + between), 0.0)
    return jnp.einsum('bhqs,bshd->bqhd', w.astype(v.dtype), v)


def sb_prompt_attend(q, k, v, bias, k_ok):
    b, n, h, d = q.shape
    nb = n // BLOCK
    q_blocks = jnp.moveaxis(q.reshape(b, nb, BLOCK, h, d), 1, 0)
    pos_blocks = jnp.arange(n, dtype=jnp.int32).reshape(nb, BLOCK)
    o = lax.map(lambda qp: stick_breaking_block(qp[0], k, v, qp[1], k_ok, bias), (q_blocks, pos_blocks))
    return jnp.moveaxis(o, 0, 1).reshape(b, n, h, d)


def sb_sample_attend(q, k, v, bias, k_cache, v_cache, page_table):
    b, n, h, d = q.shape
    past = page_table.shape[1] * k_cache.shape[1]
    k_past = k_cache[page_table].reshape(b, past, h, d).astype(k.dtype)
    v_past = v_cache[page_table].reshape(b, past, h, d).astype(v.dtype)
    k_all = jnp.concatenate([k_past, k], axis=1)
    v_all = jnp.concatenate([v_past, v], axis=1)
    q_pos = past + jnp.arange(n, dtype=jnp.int32)
    k_ok = jnp.ones((past + n,), dtype=bool)
    return stick_breaking_block(q, k_all, v_all, q_pos, k_ok, bias)


def ssd_chunk(state, inp, a_neg, d_skip):
    x, dt, bm, cm = inp
    q = x.shape[1]
    cum = jnp.cumsum(dt * a_neg, axis=1)
    causal = jnp.tril(jnp.ones((q, q), dtype=bool))[None, :, :, None, None]
    seg = cum[:, :, None] - cum[:, None]
    decay = jnp.exp(jnp.where(causal, seg, -jnp.inf))
    xdt = x * dt[..., None]
    cb = jnp.einsum('btgn,bsgn->btsg', cm, bm)
    y = jnp.einsum('btsg,btsgh,bsghp->btghp', cb, decay, xdt)
    y = y + jnp.einsum('btgn,bghpn->btghp', cm, state) * jnp.exp(cum)[..., None]
    to_end = jnp.exp(cum[:, -1:] - cum)
    new_state = (state * jnp.exp(cum[:, -1])[..., None, None]
                 + jnp.einsum('bsgn,bsgh,bsghp->bghpn', bm, to_end, xdt))
    y = y + x * d_skip[..., None]
    return new_state, y


def ssd_scan(x, dt, bm, cm, state0, a_neg, d_skip):
    b, n = x.shape[:2]
    q = BLOCK if n % BLOCK == 0 else n
    nc = n // q

    def chunks(t):
        return jnp.moveaxis(t.reshape((b, nc, q) + t.shape[2:]), 1, 0)

    state, y = lax.scan(functools.partial(ssd_chunk, a_neg=a_neg, d_skip=d_skip), state0,
                        (chunks(x), chunks(dt), chunks(bm), chunks(cm)))
    return jnp.moveaxis(y, 0, 1).reshape(x.shape), state


def mixer(xn, valid, conv_hist, ssm_state0, attend, w_in, sb_bias, conv_w, conv_b, dt_bias, a_log, d_skip,
          ssm_norm_w, w_sb_out, w_ssm_out, w_out):
    b, n, _ = xn.shape
    q, k, v, z, xbc, dt_raw, g_raw = jnp.split(xn @ w_in, IN_SPLITS, axis=-1)
    q = q.reshape(b, n, SB_HEADS, SB_HEAD_DIM)
    k = k.reshape(b, n, SB_HEADS, SB_HEAD_DIM)
    v = v.reshape(b, n, SB_HEADS, SB_HEAD_DIM)
    o_sb = attend(q, k, v, sb_bias).reshape(b, n, SB_WIDTH)
    xbc = xbc * valid[None, :, None]
    hist = jnp.concatenate([conv_hist.astype(xbc.dtype), xbc], axis=1)
    xbc_c = jax.nn.silu(causal_dwconv(hist, conv_w, conv_b))
    xs, bm, cm = jnp.split(xbc_c, [SSM_INNER, SSM_INNER + SSM_GROUPS * SSM_STATE], axis=-1)
    xs = xs.reshape(b, n, SSM_GROUPS, SSM_HEADS_PER_GROUP, SSM_HEAD_DIM)
    bm = bm.reshape(b, n, SSM_GROUPS, SSM_STATE)
    cm = cm.reshape(b, n, SSM_GROUPS, SSM_STATE)
    dt = jax.nn.softplus(dt_raw.astype(jnp.float32) + dt_bias.astype(jnp.float32)) * valid[None, :, None]
    dt = dt.reshape(b, n, SSM_GROUPS, SSM_HEADS_PER_GROUP)
    a_neg = -jnp.exp(a_log.astype(jnp.float32)).reshape(SSM_GROUPS, SSM_HEADS_PER_GROUP)
    state0 = ssm_state0.astype(jnp.float32).reshape(b, SSM_GROUPS, SSM_HEADS_PER_GROUP, SSM_HEAD_DIM, SSM_STATE)
    y, state = ssd_scan(xs, dt, bm, cm, state0, a_neg,
                        d_skip.reshape(SSM_GROUPS, SSM_HEADS_PER_GROUP))
    y = y.reshape(b, n, SSM_INNER).astype(jnp.float32) * jax.nn.silu(z.astype(jnp.float32))
    yg = y.reshape(b, n, SSM_GROUPS, SSM_INNER // SSM_GROUPS)
    yg = yg * lax.rsqrt(jnp.mean(yg * yg, axis=-1, keepdims=True) + EPS)
    y = (yg.reshape(b, n, SSM_INNER) * ssm_norm_w.astype(jnp.float32)).astype(xn.dtype)
    g_a, g_s = jnp.split(jax.nn.sigmoid(g_raw), 2, axis=-1)
    merged = g_a * (o_sb @ w_sb_out) + g_s * (y @ w_ssm_out)
    out = merged @ w_out
    state = state.reshape(b, SSM_HEADS, SSM_HEAD_DIM, SSM_STATE)
    return out, k, v, state, hist[:, -(SSM_CONV - 1):]


def ffn_block(xn, valid, hist, w_up, conv_w, conv_b, w_down):
    gate, val = jnp.split(xn @ w_up, 2, axis=-1)
    gate = gate * valid[None, :, None]
    g_hist = jnp.concatenate([hist.astype(gate.dtype), gate], axis=1)
    hdn = jax.nn.silu(causal_dwconv(g_hist, conv_w, conv_b)) * val
    return hdn @ w_down, g_hist[:, -(FFN_CONV - 1):]


def decoder_layer(h, valid, conv_hist, ssm_state0, ffn_hist, attend, norm1_w, w_in, sb_bias, conv_w, conv_b,
                  dt_bias, a_log, d_skip, ssm_norm_w, w_sb_out, w_ssm_out, w_out, norm2_w, w_ffn_up,
                  ffn_conv_w, ffn_conv_b, w_ffn_down):
    out, k, v, ssm_state, conv_state = mixer(rmsnorm(h, norm1_w), valid, conv_hist, ssm_state0, attend,
                                             w_in, sb_bias, conv_w, conv_b, dt_bias, a_log, d_skip, ssm_norm_w,
                                             w_sb_out, w_ssm_out, w_out)
    h = h + out
    f, ffn_state = ffn_block(rmsnorm(h, norm2_w), valid, ffn_hist, w_ffn_up, ffn_conv_w, ffn_conv_b, w_ffn_down)
    return h + f, k, v, ssm_state, conv_state, ffn_state


def setup_inputs(seed: int = 0) -> dict:
    key = jax.random.key(seed)
    ks = jax.random.split(key, 32)
    f32 = jnp.float32
    n_pages = PAST_LEN // PAGE_SIZE
    n_used = DEC_BATCH * n_pages
    n_pool = n_used + (n_used + 3) // 4
    nrm = lambda k, shape, s: jax.random.normal(k, shape, f32) * s
    dt0 = jnp.exp(jax.random.uniform(ks[12], (DEPTH, SSM_HEADS), f32) * (jnp.log(0.1) - jnp.log(0.001)) + jnp.log(0.001))
    return {
        'x_prompt': nrm(ks[0], (BATCH, SEQ, D_MODEL), 1.0),
        'x_sample': nrm(ks[1], (DEC_BATCH, DEC_SEQ, D_MODEL), 1.0),
        'cache_k': nrm(ks[2], (DEPTH, n_pool, PAGE_SIZE, SB_HEADS, SB_HEAD_DIM), 1.0),
        'cache_v': nrm(ks[3], (DEPTH, n_pool, PAGE_SIZE, SB_HEADS, SB_HEAD_DIM), 1.0),
        'state_ssm': nrm(ks[4], (DEPTH, DEC_BATCH, SSM_HEADS, SSM_HEAD_DIM, SSM_STATE), 0.1),
        'state_conv': nrm(ks[5], (DEPTH, DEC_BATCH, SSM_CONV - 1, CONV_DIM), 1.0),
        'state_ffn_conv': nrm(ks[6], (DEPTH, DEC_BATCH, FFN_CONV - 1, FFN_DIM), 1.0),
        'page_table': jax.random.permutation(ks[7], n_pool)[:n_used].reshape(DEC_BATCH, n_pages).astype(jnp.int32),
        'meta_tokens': nrm(ks[8], (N_META, D_MODEL), 1.0),
        'norm1_w': 1.0 + nrm(ks[9], (DEPTH, D_MODEL), 0.1),
        'w_in': nrm(ks[10], (DEPTH, D_MODEL, IN_WIDTH), D_MODEL ** -0.5),
        'sb_logit_bias': SB_BIAS_INIT + nrm(ks[26], (DEPTH, SB_HEADS), 0.5),
        'conv_w': nrm(ks[11], (DEPTH, SSM_CONV, CONV_DIM), 0.5),
        'conv_b': nrm(ks[13], (DEPTH, CONV_DIM), 0.1),
        'dt_bias': dt0 + jnp.log(-jnp.expm1(-dt0)),
        'a_log': jnp.log(jax.random.uniform(ks[14], (DEPTH, SSM_HEADS), f32, 1.0, 16.0)),
        'd_skip': 1.0 + nrm(ks[15], (DEPTH, SSM_HEADS), 0.1),
        'ssm_norm_w': 1.0 + nrm(ks[16], (DEPTH, SSM_INNER), 0.1),
        'w_sb_out': nrm(ks[17], (DEPTH, SB_WIDTH, D_MODEL), SB_WIDTH ** -0.5),
        'w_ssm_out': nrm(ks[18], (DEPTH, SSM_INNER, D_MODEL), SSM_INNER ** -0.5),
        'w_out': nrm(ks[19], (DEPTH, D_MODEL, D_MODEL), D_MODEL ** -0.5),
        'norm2_w': 1.0 + nrm(ks[20], (DEPTH, D_MODEL), 0.1),
        'w_ffn_up': nrm(ks[21], (DEPTH, D_MODEL, 2 * FFN_DIM), D_MODEL ** -0.5),
        'ffn_conv_w': nrm(ks[22], (DEPTH, FFN_CONV, FFN_DIM), FFN_CONV ** -0.5),
        'ffn_conv_b': nrm(ks[23], (DEPTH, FFN_DIM), 0.1),
        'w_ffn_down': nrm(ks[24], (DEPTH, FFN_DIM, D_MODEL), FFN_DIM ** -0.5),
        'final_norm_w': 1.0 + nrm(ks[25], (D_MODEL,), 0.1),
    }


def reference(x_prompt, x_sample, cache_k, cache_v, state_ssm, state_conv, state_ffn_conv, page_table,
              meta_tokens, norm1_w, w_in, sb_logit_bias, conv_w, conv_b, dt_bias, a_log, d_skip, ssm_norm_w,
              w_sb_out, w_ssm_out, w_out, norm2_w, w_ffn_up, ffn_conv_w, ffn_conv_b, w_ffn_down, final_norm_w):
    dtype = x_prompt.dtype
    b, s, _ = x_prompt.shape
    pad = BLOCK - N_META
    n = s + BLOCK
    hp = jnp.concatenate([jnp.zeros((b, pad, D_MODEL), dtype),
                          jnp.broadcast_to(meta_tokens.astype(dtype)[None], (b, N_META, D_MODEL)),
                          x_prompt], axis=1)
    key_ok_p = jnp.arange(n) >= pad
    valid_p = key_ok_p.astype(dtype)
    hs = x_sample
    valid_s = jnp.ones((hs.shape[1],), dtype)
    kp_l, vp_l, sp_l, cp_l, fp_l = [], [], [], [], []
    ks_l, vs_l, ss_l, cs_l, fs_l = [], [], [], [], []
    for l in range(DEPTH):
        lp = (norm1_w[l], w_in[l], sb_logit_bias[l], conv_w[l], conv_b[l], dt_bias[l], a_log[l], d_skip[l],
              ssm_norm_w[l], w_sb_out[l], w_ssm_out[l], w_out[l], norm2_w[l], w_ffn_up[l], ffn_conv_w[l],
              ffn_conv_b[l], w_ffn_down[l])
        hp, k, v, st, cv, fc = decoder_layer(
            hp, valid_p, jnp.zeros((b, SSM_CONV - 1, CONV_DIM), dtype),
            jnp.zeros((b, SSM_HEADS, SSM_HEAD_DIM, SSM_STATE), jnp.float32),
            jnp.zeros((b, FFN_CONV - 1, FFN_DIM), dtype),
            functools.partial(sb_prompt_attend, k_ok=key_ok_p), *lp)
        kp_l.append(k[:, pad:]); vp_l.append(v[:, pad:]); sp_l.append(st); cp_l.append(cv); fp_l.append(fc)
        hs, k, v, st, cv, fc = decoder_layer(
            hs, valid_s, state_conv[l], state_ssm[l], state_ffn_conv[l],
            functools.partial(sb_sample_attend, k_cache=cache_k[l], v_cache=cache_v[l], page_table=page_table),
            *lp)
        ks_l.append(k); vs_l.append(v); ss_l.append(st); cs_l.append(cv); fs_l.append(fc)
    y_prompt = rmsnorm(hp, final_norm_w)[:, BLOCK:]
    y_sample = rmsnorm(hs, final_norm_w)
    k_prompt = jnp.stack(kp_l)
    v_prompt = jnp.stack(vp_l)
    k_sample = jnp.stack(ks_l)
    v_sample = jnp.stack(vs_l)
    ssm_prompt = jnp.stack(sp_l)
    ssm_sample = jnp.stack(ss_l)
    conv_prompt = jnp.stack(cp_l)
    conv_sample = jnp.stack(cs_l)
    ffn_prompt = jnp.stack(fp_l)
    ffn_sample = jnp.stack(fs_l)
    return (y_prompt, y_sample, k_prompt, v_prompt, k_sample, v_sample,
            ssm_prompt, ssm_sample, conv_prompt, conv_sample, ffn_prompt, ffn_sample)
```

```python
import functools

import jax
import jax.numpy as jnp
from jax import lax
from jax.experimental import pallas as pl
from jax.experimental.pallas import tpu as pltpu

F32 = jnp.float32
BF16 = jnp.bfloat16
HIGHEST = lax.Precision.HIGHEST

N_META = 16
BLOCK = 128
PAD = BLOCK - N_META
SB_HEADS = 8
SB_HEAD_DIM = 64
SB_WIDTH = SB_HEADS * SB_HEAD_DIM
SB_SCALE = SB_HEAD_DIM ** -0.5
SSM_HEADS = 16
SSM_HEAD_DIM = 64
SSM_INNER = SSM_HEADS * SSM_HEAD_DIM
SSM_GROUPS = 4
SSM_STATE = 128
SSM_CONV = 4
CONV_DIM = SSM_INNER + 2 * SSM_GROUPS * SSM_STATE
FFN_CONV = 3
EPS = 1e-6
LANES = 128
SUBLANES = 8
DT_PAD = LANES
VMEM_LIMIT = 56 * 1024 * 1024


def _cparams(sem):
    return pltpu.CompilerParams(dimension_semantics=sem, vmem_limit_bytes=VMEM_LIMIT)


def _const_spec(shape):
    nd = len(shape)
    return pl.BlockSpec(shape, lambda *_: (0,) * nd, pipeline_mode=pl.Buffered(1))


def _full_spec(shape):
    nd = len(shape)
    return pl.BlockSpec(shape, lambda *_: (0,) * nd)


def _bdot(a, b):
    return jnp.dot(a, b, preferred_element_type=F32)


def _dot_nt(a, b):
    return lax.dot_general(a, b, (((1,), (1,)), ((), ())), preferred_element_type=F32)


def _dot_tn(a, b, precision=None):
    return lax.dot_general(a, b, (((0,), (0,)), ((), ())), preferred_element_type=F32, precision=precision)


def _hdot(a, b):
    return jnp.dot(a, b, preferred_element_type=F32, precision=HIGHEST)


def _sigmoid(x):
    return 1.0 / (1.0 + jnp.exp(-x))


def _silu(x):
    return x * _sigmoid(x)


def _softplus(x):
    return jnp.maximum(x, 0.0) + jnp.log(1.0 + jnp.exp(-jnp.abs(x)))


def _rms(x, w):
    return x * lax.rsqrt(jnp.mean(x * x, axis=-1, keepdims=True) + EPS) * w


def _split_bf16(x):
    hi = x.astype(BF16)
    lo = (x - hi.astype(F32)).astype(BF16)
    return hi, lo


def _inproj_kernel(h_ref, nw_ref, w_ref, q_ref, kb_ref, vb_ref, k_ref, v_ref, xbc_ref, dt_ref, xn_ref):
    xn = _rms(h_ref[...], nw_ref[...]).astype(BF16)
    xn_ref[...] = xn
    qkv = _bdot(xn, w_ref[:, 0:3 * SB_WIDTH])
    q_ref[...] = (qkv[:, 0:SB_WIDTH] * SB_SCALE).astype(BF16)
    k = qkv[:, SB_WIDTH:2 * SB_WIDTH]
    v = qkv[:, 2 * SB_WIDTH:3 * SB_WIDTH]
    k_ref[...] = k
    v_ref[...] = v
    kb_ref[...] = k.astype(BF16)
    vb_ref[...] = v.astype(BF16)
    o = 3 * SB_WIDTH
    xbc_ref[...] = _bdot(xn, w_ref[:, o:o + CONV_DIM])
    dt_ref[...] = _bdot(xn, w_ref[:, o + CONV_DIM:o + CONV_DIM + DT_PAD])


def _inproj(h, norm_w, w_a, tm):
    m, d = h.shape
    wa = w_a.shape[1]
    row = lambda width: pl.BlockSpec((tm, width), lambda i: (i, 0))
    out_shape = (
        jax.ShapeDtypeStruct((m, SB_WIDTH), BF16),
        jax.ShapeDtypeStruct((m, SB_WIDTH), BF16),
        jax.ShapeDtypeStruct((m, SB_WIDTH), BF16),
        jax.ShapeDtypeStruct((m, SB_WIDTH), F32),
        jax.ShapeDtypeStruct((m, SB_WIDTH), F32),
        jax.ShapeDtypeStruct((m, CONV_DIM), F32),
        jax.ShapeDtypeStruct((m, DT_PAD), F32),
        jax.ShapeDtypeStruct((m, d), BF16),
    )
    return pl.pallas_call(
        _inproj_kernel,
        grid=(m // tm,),
        in_specs=[row(d), _const_spec((1, d)), _const_spec((d, wa))],
        out_specs=(row(SB_WIDTH), row(SB_WIDTH), row(SB_WIDTH), row(SB_WIDTH), row(SB_WIDTH),
                   row(CONV_DIM), row(DT_PAD), row(d)),
        out_shape=out_shape,
        compiler_params=_cparams(("parallel",)),
        name="inproj",
    )(h, norm_w, w_a)


def _sb_weights(z, lk, carry, u2):
    hi, lo = _split_bf16(lk)
    r = _bdot(hi, u2) + _bdot(lo, u2)
    w = jnp.exp(z + lk + r[:, :BLOCK] + carry)
    return w, carry + r[:, BLOCK:]


def _attn_kernel(bias_ref, q_ref, k_ref, v_ref, o_ref, acc_ref):
    pair = pl.program_id(1)
    qi = pl.program_id(2)
    lane = lax.broadcasted_iota(jnp.int32, (BLOCK, BLOCK), 1)
    row = lax.broadcasted_iota(jnp.int32, (BLOCK, BLOCK), 0)
    col2 = lax.broadcasted_iota(jnp.int32, (BLOCK, 2 * BLOCK), 1)
    row2 = lax.broadcasted_iota(jnp.int32, (BLOCK, 2 * BLOCK), 0)
    u2 = jnp.where((row2 > col2) | (col2 >= BLOCK), 1.0, 0.0).astype(BF16)
    q = q_ref[...]
    zero = jnp.zeros_like(q)
    q_heads = (jnp.where(lane < SB_HEAD_DIM, q, zero), jnp.where(lane >= SB_HEAD_DIM, q, zero))
    t_pos = qi * BLOCK + row

    for hh in range(2):
        qh = q_heads[hh]
        bias = bias_ref[2 * pair + hh]

        def block(sj, carry, masked, qh=qh, bias=bias, hh=hh):
            start = sj * BLOCK if isinstance(sj, int) else pl.multiple_of(sj * BLOCK, BLOCK)
            kb = k_ref[pl.ds(start, BLOCK), :]
            vb = v_ref[pl.ds(start, BLOCK), :]
            z = _dot_nt(qh, kb) + bias
            lk = -_softplus(z)
            if masked:
                s_pos = sj * BLOCK + lane
                mask = (s_pos < t_pos) & (s_pos >= PAD)
                lk = jnp.where(mask, lk, 0.0)
            w, carry = _sb_weights(z, lk, carry, u2)
            if masked:
                w = jnp.where(mask, w, 0.0)
            acc_ref[hh] += _bdot(w.astype(BF16), vb)
            return carry

        acc_ref[hh] = jnp.zeros((BLOCK, BLOCK), F32)
        carry = block(qi, jnp.zeros((BLOCK, BLOCK), F32), True)
        n_inner = jnp.maximum(qi - 1, 0)
        carry = lax.fori_loop(0, n_inner, lambda i, c: block(qi - 1 - i, c, False), carry)

        @pl.when(qi > 0)
        def _():
            block(0, carry, True)

    o_ref[...] = jnp.where(lane < SB_HEAD_DIM, acc_ref[0], acc_ref[1]).astype(o_ref.dtype)


def _sb_attn_prompt(q, k, v, bias):
    b, n, _ = q.shape
    nb = n // BLOCK
    return pl.pallas_call(
        _attn_kernel,
        grid=(b, SB_WIDTH // LANES, nb),
        in_specs=[
            pl.BlockSpec(memory_space=pltpu.SMEM),
            pl.BlockSpec((None, BLOCK, LANES), lambda bi, p, qi: (bi, qi, p)),
            pl.BlockSpec((None, n, LANES), lambda bi, p, qi: (bi, 0, p)),
            pl.BlockSpec((None, n, LANES), lambda bi, p, qi: (bi, 0, p)),
        ],
        out_specs=pl.BlockSpec((None, BLOCK, LANES), lambda bi, p, qi: (bi, qi, p)),
        scratch_shapes=[pltpu.VMEM((2, BLOCK, BLOCK), F32)],
        out_shape=jax.ShapeDtypeStruct((b, n, SB_WIDTH), BF16),
        compiler_params=_cparams(("parallel", "parallel", "arbitrary")),
        name="sb_attn_prompt",
    )(bias, q, k, v)


def _ssd_kernel(xbc_ref, dt_ref, cw_ref, cb_ref, dtb_ref, aneg_ref, dskip_ref, e_ref, et_ref,
                y_ref, st_ref, cs_ref, hist_ref):
    c = pl.program_id(1)
    row = lax.broadcasted_iota(jnp.int32, (BLOCK, 1), 0)
    valid = jnp.where((c > 0) | (row >= PAD), 1.0, 0.0)

    @pl.when(c == 0)
    def _():
        hist_ref[0:SUBLANES, :] = jnp.zeros((SUBLANES, CONV_DIM), F32)
        st_ref[...] = jnp.zeros_like(st_ref)

    x = xbc_ref[...] * valid
    hist_ref[SUBLANES:SUBLANES + BLOCK, :] = x
    conv = cb_ref[...] + x * cw_ref[SSM_CONV - 1:SSM_CONV, :]
    for j in range(SSM_CONV - 1):
        back = SSM_CONV - 1 - j
        conv = conv + hist_ref[SUBLANES - back:SUBLANES - back + BLOCK, :] * cw_ref[j:j + 1, :]
    tail = x[BLOCK - SUBLANES:, :]
    hist_ref[0:SUBLANES, :] = tail
    cs_ref[...] = tail
    xc = _silu(conv)
    xs = xc[:, :SSM_INNER]
    bm = xc[:, SSM_INNER:SSM_INNER + SSM_GROUPS * SSM_STATE].astype(BF16)
    cm = xc[:, SSM_INNER + SSM_GROUPS * SSM_STATE:].astype(BF16)

    dt = _softplus(dt_ref[...] + dtb_ref[...]) * valid
    la = dt * aneg_ref[...]
    ri = lax.broadcasted_iota(jnp.int32, (BLOCK, BLOCK), 0)
    ci = lax.broadcasted_iota(jnp.int32, (BLOCK, BLOCK), 1)
    causal = ci <= ri
    tri = jnp.where(causal, 1.0, 0.0)
    cum = _hdot(tri, la)
    cum_t = cum.T
    e = e_ref[...]
    xdt = xs * _hdot(dt, e)
    ecum = _hdot(jnp.exp(cum), e)
    cum_last = cum[BLOCK - 1:BLOCK, :]
    xdtw = (xdt * _hdot(jnp.exp(cum_last - cum), e)).astype(BF16)
    xdt_b = xdt.astype(BF16)
    ecl = jnp.exp(cum_t[:, BLOCK - 1:BLOCK])
    fac = _hdot(et_ref[...], jnp.broadcast_to(ecl, (DT_PAD, SSM_STATE)))
    lane = ci

    for g in range(SSM_GROUPS):
        cmg = cm[:, g * SSM_STATE:(g + 1) * SSM_STATE]
        bmg = bm[:, g * SSM_STATE:(g + 1) * SSM_STATE]
        cb = _dot_nt(cmg, bmg)
        for pp in range(2):
            pair = 2 * g + pp
            cols = slice(pair * LANES, (pair + 1) * LANES)
            ys = []
            for hh in range(2):
                h = 2 * pair + hh
                seg = cum[:, h:h + 1] - cum_t[h:h + 1, :]
                m = (cb * jnp.where(causal, jnp.exp(seg), 0.0)).astype(BF16)
                ys.append(_bdot(m, xdt_b[:, cols]))
            y = jnp.where(lane < SSM_HEAD_DIM, ys[0], ys[1])
            st = st_ref[cols, :]
            y = y + _dot_nt(cmg, st.astype(BF16)) * ecum[:, cols]
            st_ref[cols, :] = st * fac[cols, :] + _dot_tn(xdtw[:, cols], bmg)
            y_ref[:, cols] = y + xs[:, cols] * dskip_ref[:, cols]


def _ssd_prompt(xbc, dt, b, cw, cb, dtb, aneg, dskip, e, et):
    m = xbc.shape[0]
    nc = m // b // BLOCK
    row = lambda width: pl.BlockSpec((BLOCK, width), lambda bi, c: (bi * nc + c, 0))
    out_shape = (
        jax.ShapeDtypeStruct((m, SSM_INNER), F32),
        jax.ShapeDtypeStruct((b, SSM_INNER, SSM_STATE), F32),
        jax.ShapeDtypeStruct((b, SUBLANES, CONV_DIM), F32),
    )
    return pl.pallas_call(
        _ssd_kernel,
        grid=(b, nc),
        in_specs=[row(CONV_DIM), row(DT_PAD), _const_spec(cw.shape), _const_spec(cb.shape),
                  _const_spec(dtb.shape), _const_spec(aneg.shape), _const_spec(dskip.shape),
                  _const_spec(e.shape), _const_spec(et.shape)],
        out_specs=(row(SSM_INNER),
                   pl.BlockSpec((None, SSM_INNER, SSM_STATE), lambda bi, c: (bi, 0, 0)),
                   pl.BlockSpec((None, SUBLANES, CONV_DIM), lambda bi, c: (bi, 0, 0))),
        out_shape=out_shape,
        scratch_shapes=[pltpu.VMEM((SUBLANES + BLOCK, CONV_DIM), F32)],
        compiler_params=_cparams(("parallel", "arbitrary")),
        name="ssd_prompt",
    )(xbc, dt, cw, cb, dtb, aneg, dskip, e, et)


def _mix_kernel(xn_ref, h_ref, osb_ref, y_ref, wz_ref, wg_ref, wsb_ref, wssm_ref, wo_ref, snw_ref, n2w_ref,
                h1_ref, xn2_ref):
    xn = xn_ref[...]
    d = h_ref.shape[1]
    z = _bdot(xn, wz_ref[...])
    yz = y_ref[...] * _silu(z)
    gw = SSM_INNER // SSM_GROUPS
    parts = []
    for g in range(SSM_GROUPS):
        yg = yz[:, g * gw:(g + 1) * gw]
        parts.append(yg * lax.rsqrt(jnp.mean(yg * yg, axis=-1, keepdims=True) + EPS))
    yn = (jnp.concatenate(parts, axis=1) * snw_ref[...]).astype(BF16)
    g_a = _sigmoid(_bdot(xn, wg_ref[:, 0:d]))
    merged = g_a * _bdot(osb_ref[...], wsb_ref[...])
    g_s = _sigmoid(_bdot(xn, wg_ref[:, d:2 * d]))
    merged = merged + g_s * _bdot(yn, wssm_ref[...])
    h1 = h_ref[...] + _bdot(merged.astype(BF16), wo_ref[...])
    h1_ref[...] = h1
    xn2_ref[...] = _rms(h1, n2w_ref[...]).astype(BF16)


def _mix_out(xn, h, osb, y, wz, wg, wsb, wssm, wo, snw, n2w, tm):
    m, d = h.shape
    row = lambda width: pl.BlockSpec((tm, width), lambda i: (i, 0))
    return pl.pallas_call(
        _mix_kernel,
        grid=(m // tm,),
        in_specs=[row(d), row(d), row(SB_WIDTH), row(SSM_INNER)] +
                 [_const_spec(a.shape) for a in (wz, wg, wsb, wssm, wo, snw, n2w)],
        out_specs=(row(d), row(d)),
        out_shape=(jax.ShapeDtypeStruct((m, d), F32), jax.ShapeDtypeStruct((m, d), BF16)),
        compiler_params=_cparams(("parallel",)),
        name="mix_out",
    )(xn, h, osb, y, wz, wg, wsb, wssm, wo, snw, n2w)


def _ffn_tail(hdn, h1, wdn_ref, fw_ref, out_ref, final):
    h2 = h1 + _bdot(hdn.astype(BF16), wdn_ref[...])
    out_ref[...] = _rms(h2, fw_ref[...]) if final else h2


def _ffn_prompt_kernel(xn2_ref, h1_ref, wup_ref, wdn_ref, cw_ref, cb_ref, fw_ref, out_ref, fs_ref, hist_ref,
                       *, final):
    t = pl.program_id(1)
    tm = xn2_ref.shape[0]
    f = cw_ref.shape[1]

    @pl.when(t == 0)
    def _():
        hist_ref[0:SUBLANES, :] = jnp.zeros((SUBLANES, f), F32)

    row = lax.broadcasted_iota(jnp.int32, (tm, 1), 0)
    valid = jnp.where((t > 0) | (row >= PAD), 1.0, 0.0)
    xn2 = xn2_ref[...]
    gate = _bdot(xn2, wup_ref[:, 0:f]) * valid
    hist_ref[SUBLANES:SUBLANES + tm, :] = gate
    conv = cb_ref[...] + gate * cw_ref[FFN_CONV - 1:FFN_CONV, :]
    for j in range(FFN_CONV - 1):
        back = FFN_CONV - 1 - j
        conv = conv + hist_ref[SUBLANES - back:SUBLANES - back + tm, :] * cw_ref[j:j + 1, :]
    tail = gate[tm - SUBLANES:, :]
    hist_ref[0:SUBLANES, :] = tail
    fs_ref[...] = tail
    hdn = _silu(conv) * _bdot(xn2, wup_ref[:, f:2 * f])
    _ffn_tail(hdn, h1_ref[...], wdn_ref, fw_ref, out_ref, final)


def _ffn_prompt(xn2, h1, b, wup, wdn, cw, cb, fw, tm, final):
    m, d = h1.shape
    f = cw.shape[1]
    nt = m // b // tm
    row = lambda width: pl.BlockSpec((tm, width), lambda bi, t: (bi * nt + t, 0))
    return pl.pallas_call(
        functools.partial(_ffn_prompt_kernel, final=final),
        grid=(b, nt),
        in_specs=[row(d), row(d)] + [_const_spec(a.shape) for a in (wup, wdn, cw, cb, fw)],
        out_specs=(row(d), pl.BlockSpec((None, SUBLANES, f), lambda bi, t: (bi, 0, 0))),
        out_shape=(jax.ShapeDtypeStruct((m, d), F32), jax.ShapeDtypeStruct((b, SUBLANES, f), F32)),
        scratch_shapes=[pltpu.VMEM((SUBLANES + tm, f), F32)],
        compiler_params=_cparams(("parallel", "arbitrary")),
        name="ffn_prompt",
    )(xn2, h1, wup, wdn, cw, cb, fw)


def _ffn_sample_kernel(xn2_ref, h1_ref, hist_ref, wup_ref, wdn_ref, cw_ref, cb_ref, fw_ref, out_ref, fs_ref,
                       *, final):
    f = cw_ref.shape[1]
    xn2 = xn2_ref[...]
    gate = _bdot(xn2, wup_ref[:, 0:f])
    conv = cb_ref[...] + gate * cw_ref[FFN_CONV - 1:FFN_CONV, :]
    for j in range(FFN_CONV - 1):
        conv = conv + hist_ref[:, j * f:(j + 1) * f] * cw_ref[j:j + 1, :]
    for j in range(FFN_CONV - 2):
        fs_ref[:, j * f:(j + 1) * f] = hist_ref[:, (j + 1) * f:(j + 2) * f]
    fs_ref[:, (FFN_CONV - 2) * f:(FFN_CONV - 1) * f] = gate
    hdn = _silu(conv) * _bdot(xn2, wup_ref[:, f:2 * f])
    _ffn_tail(hdn, h1_ref[...], wdn_ref, fw_ref, out_ref, final)


def _ffn_sample(xn2, h1, hist, wup, wdn, cw, cb, fw, final):
    m, d = h1.shape
    return pl.pallas_call(
        functools.partial(_ffn_sample_kernel, final=final),
        grid=(1,),
        in_specs=[_const_spec(a.shape) for a in (xn2, h1, hist, wup, wdn, cw, cb, fw)],
        out_specs=(_full_spec((m, d)), _full_spec(hist.shape)),
        out_shape=(jax.ShapeDtypeStruct((m, d), F32), jax.ShapeDtypeStruct(hist.shape, F32)),
        compiler_params=_cparams(("arbitrary",)),
        name="ffn_sample",
    )(xn2, h1, hist, wup, wdn, cw, cb, fw)


def _attn_sample_kernel(pt_ref, q_ref, bias_ref, *refs, n_pages):
    k_refs = refs[:n_pages]
    v_refs = refs[n_pages:2 * n_pages]
    o_ref = refs[2 * n_pages]
    hrow = lax.broadcasted_iota(jnp.int32, (SB_HEADS, SB_WIDTH), 0)
    hcol = lax.broadcasted_iota(jnp.int32, (SB_HEADS, SB_WIDTH), 1) // SB_HEAD_DIM
    head_mask = hrow == hcol
    qm = jnp.where(head_mask, jnp.broadcast_to(q_ref[...], (SB_HEADS, SB_WIDTH)), 0.0).astype(BF16)
    bias = bias_ref[...]
    z = jnp.concatenate([_dot_nt(qm, k_refs[c][...].astype(BF16)) + bias for c in range(n_pages)], axis=0)
    lk = -_softplus(z)
    nr = n_pages * SB_HEADS
    row2 = lax.broadcasted_iota(jnp.int32, (BLOCK, 2 * BLOCK), 0)
    col2 = lax.broadcasted_iota(jnp.int32, (BLOCK, 2 * BLOCK), 1)
    u2 = jnp.where((row2 > col2) | (col2 >= BLOCK), 1.0, 0.0).astype(BF16)
    hi, lo = _split_bf16(lk)
    r = _bdot(hi, u2) + _bdot(lo, u2)
    ra = lax.broadcasted_iota(jnp.int32, (nr, nr), 0)
    rb = lax.broadcasted_iota(jnp.int32, (nr, nr), 1)
    mx = jnp.where((ra % SB_HEADS == rb % SB_HEADS) & (rb // SB_HEADS > ra // SB_HEADS), 1.0, 0.0).astype(BF16)
    thi, tlo = _split_bf16(r[:, BLOCK:])
    later = _bdot(mx, thi) + _bdot(mx, tlo)
    w = jnp.exp(z + lk + r[:, :BLOCK] + later)
    o_full = jnp.zeros((SB_HEADS, SB_WIDTH), F32)
    for c in range(n_pages):
        wc = w[c * SB_HEADS:(c + 1) * SB_HEADS, :].astype(BF16)
        o_full = o_full + _bdot(wc, v_refs[c][...].astype(BF16))
    o_ref[...] = jnp.sum(jnp.where(head_mask, o_full, 0.0), axis=0, keepdims=True)


def _sb_attn_sample(q, bias_b, cache_k, cache_v, page_table, layer):
    s = q.shape[0]
    n_pages = page_table.shape[1]
    page = cache_k.shape[2]
    assert page == BLOCK and n_pages * SB_HEADS == BLOCK
    q3 = q.reshape(s, 1, SB_WIDTH)

    def page_spec(c):
        return pl.BlockSpec((None, None, page, SB_WIDTH), lambda i, pt, c=c: (layer, pt[i * n_pages + c], 0, 0))

    grid_spec = pltpu.PrefetchScalarGridSpec(
        num_scalar_prefetch=1,
        grid=(s,),
        in_specs=[pl.BlockSpec((None, 1, SB_WIDTH), lambda i, pt: (i, 0, 0)),
                  pl.BlockSpec((SB_HEADS, BLOCK), lambda i, pt: (0, 0))] +
                 [page_spec(c) for c in range(n_pages)] * 2,
        out_specs=pl.BlockSpec((None, 1, SB_WIDTH), lambda i, pt: (i, 0, 0)),
    )
    out = pl.pallas_call(
        functools.partial(_attn_sample_kernel, n_pages=n_pages),
        grid_spec=grid_spec,
        out_shape=jax.ShapeDtypeStruct((s, 1, SB_WIDTH), F32),
        compiler_params=_cparams(("arbitrary",)),
        name="sb_attn_sample",
    )(page_table.reshape(-1), q3, bias_b, *([cache_k] * n_pages), *([cache_v] * n_pages))
    return out.reshape(s, SB_WIDTH)


def _ssd_sample_kernel(xbc_ref, dt_ref, cs_ref, st_ref, cw_ref, cb_ref, dtb_ref, aneg_ref, dskip_ref, e_ref,
                       y_ref, nst_ref, ncs_ref):
    sb = xbc_ref.shape[0]
    x = xbc_ref[...]
    conv = cb_ref[...] + x * cw_ref[SSM_CONV - 1:SSM_CONV, :]
    for j in range(SSM_CONV - 1):
        conv = conv + cs_ref[:, j * CONV_DIM:(j + 1) * CONV_DIM] * cw_ref[j:j + 1, :]
    for j in range(SSM_CONV - 2):
        ncs_ref[:, j * CONV_DIM:(j + 1) * CONV_DIM] = cs_ref[:, (j + 1) * CONV_DIM:(j + 2) * CONV_DIM]
    ncs_ref[:, (SSM_CONV - 2) * CONV_DIM:(SSM_CONV - 1) * CONV_DIM] = x
    xc = _silu(conv)
    xs = xc[:, :SSM_INNER]
    bm = xc[:, SSM_INNER:SSM_INNER + SSM_GROUPS * SSM_STATE]
    cm = xc[:, SSM_INNER + SSM_GROUPS * SSM_STATE:]
    dt = _softplus(dt_ref[...] + dtb_ref[...])
    e = e_ref[...]
    xdt = xs * _hdot(dt, e)
    dec = _hdot(jnp.exp(dt * aneg_ref[...]), e)
    gw = SSM_INNER // SSM_GROUPS
    cb_parts = []
    for g in range(SSM_GROUPS):
        sl = slice(g * SSM_STATE, (g + 1) * SSM_STATE)
        cbg = jnp.sum(cm[:, sl] * bm[:, sl], axis=-1, keepdims=True)
        cb_parts.append(jnp.broadcast_to(cbg, (sb, gw)))
    y = jnp.concatenate(cb_parts, axis=1) * xdt + xs * dskip_ref[...]

    grow = lax.broadcasted_iota(jnp.int32, (SUBLANES, SSM_INNER), 0)
    gcol = lax.broadcasted_iota(jnp.int32, (SUBLANES, SSM_INNER), 1) // gw
    gmask = grow == gcol
    r8 = lax.broadcasted_iota(jnp.int32, (SUBLANES, SSM_STATE), 0)
    ones0 = jnp.where(r8 == 0, 1.0, 0.0)
    row8 = lax.broadcasted_iota(jnp.int32, (SUBLANES, SSM_INNER), 0)
    rsb = lax.broadcasted_iota(jnp.int32, (sb, SSM_INNER), 0)
    ystate = jnp.zeros((sb, SSM_INNER), F32)

    def group_rows(v, i):
        out = jnp.zeros((SUBLANES, SSM_STATE), F32)
        for g in range(SSM_GROUPS):
            vg = jnp.broadcast_to(v[i:i + 1, g * SSM_STATE:(g + 1) * SSM_STATE], (SUBLANES, SSM_STATE))
            out = jnp.where(r8 == g, vg, out)
        return out

    for i in range(sb):
        st = st_ref[i]
        y8 = _dot_nt(group_rows(cm, i).astype(BF16), st.astype(BF16))
        ys = jnp.sum(jnp.where(gmask, y8, 0.0), axis=0, keepdims=True)
        ystate = jnp.where(rsb == i, jnp.broadcast_to(ys, (sb, SSM_INNER)), ystate)
        xg = jnp.where(gmask, jnp.broadcast_to(xdt[i:i + 1, :], (SUBLANES, SSM_INNER)), 0.0)
        d8 = jnp.where(row8 == 0, jnp.broadcast_to(dec[i:i + 1, :], (SUBLANES, SSM_INNER)), 0.0)
        fac = _dot_tn(d8, ones0, precision=HIGHEST)
        inc = _dot_tn(xg, group_rows(bm, i), precision=HIGHEST)
        nst_ref[i] = st * fac + inc
    y_ref[...] = y + ystate * dec


def _ssd_sample(xbc, dt, cs, state, layer, cw, cb, dtb, aneg, dskip, e, sb):
    s = xbc.shape[0]
    row = lambda width: pl.BlockSpec((sb, width), lambda i: (i, 0))
    cs_w = cs.shape[1]
    return pl.pallas_call(
        _ssd_sample_kernel,
        grid=(s // sb,),
        in_specs=[row(CONV_DIM), row(DT_PAD), row(cs_w),
                  pl.BlockSpec((None, sb, SSM_INNER, SSM_STATE), lambda i: (layer, i, 0, 0))] +
                 [_const_spec(a.shape) for a in (cw, cb, dtb, aneg, dskip, e)],
        out_specs=(row(SSM_INNER),
                   pl.BlockSpec((sb, SSM_INNER, SSM_STATE), lambda i: (i, 0, 0)),
                   row(cs_w)),
        out_shape=(jax.ShapeDtypeStruct((s, SSM_INNER), F32),
                   jax.ShapeDtypeStruct((s, SSM_INNER, SSM_STATE), F32),
                   jax.ShapeDtypeStruct((s, cs_w), F32)),
        compiler_params=_cparams(("parallel",)),
        name="ssd_sample",
    )(xbc, dt, cs, state, cw, cb, dtb, aneg, dskip, e)


def _pad_lanes(v, width):
    return jnp.pad(v.astype(F32), (0, width - v.shape[0])).reshape(1, width)


def kernel(x_prompt, x_sample, cache_k, cache_v, state_ssm, state_conv, state_ffn_conv, page_table,
           meta_tokens, norm1_w, w_in, sb_logit_bias, conv_w, conv_b, dt_bias, a_log, d_skip, ssm_norm_w,
           w_sb_out, w_ssm_out, w_out, norm2_w, w_ffn_up, ffn_conv_w, ffn_conv_b, w_ffn_down, final_norm_w):
    b, s, d = x_prompt.shape
    depth = w_in.shape[0]
    n = s + BLOCK
    ns = x_sample.shape[0]
    f = ffn_conv_w.shape[2]
    assert x_sample.shape[1] == 1 and n % BLOCK == 0
    tm_p = 512
    tm_f = n // 8
    assert (b * n) % tm_p == 0 and tm_f % SUBLANES == 0

    hp = jnp.concatenate([jnp.zeros((b, PAD, d), F32),
                          jnp.broadcast_to(meta_tokens.astype(F32)[None], (b, N_META, d)),
                          x_prompt], axis=1).reshape(b * n, d)
    hs = x_sample.reshape(ns, d)

    hid = jnp.arange(SSM_INNER, dtype=jnp.int32) // SSM_HEAD_DIM
    e_mat = (jnp.arange(DT_PAD, dtype=jnp.int32)[:, None] == hid[None, :]).astype(F32)
    et_mat = e_mat.T
    cache_k4 = cache_k.reshape(cache_k.shape[:3] + (SB_WIDTH,))
    cache_v4 = cache_v.reshape(cache_v.shape[:3] + (SB_WIDTH,))
    state4 = state_ssm.reshape(depth, ns, SSM_INNER, SSM_STATE)

    o_qkv = 3 * SB_WIDTH
    o_z = o_qkv
    o_xbc = o_z + SSM_INNER
    o_dt = o_xbc + CONV_DIM
    o_g = o_dt + SSM_HEADS

    outs = {k: [] for k in ("kp", "vp", "sp", "cp", "fp", "ks", "vs", "ss", "cs", "fs")}
    for l in range(depth):
        final = l == depth - 1
        wl = w_in[l]
        w_a = jnp.concatenate([wl[:, :o_qkv], wl[:, o_xbc:o_dt],
                               jnp.pad(wl[:, o_dt:o_g], ((0, 0), (0, DT_PAD - SSM_HEADS)))], axis=1).astype(BF16)
        wz = wl[:, o_z:o_xbc].astype(BF16)
        wg = wl[:, o_g:].astype(BF16)
        wsb = w_sb_out[l].astype(BF16)
        wssm = w_ssm_out[l].astype(BF16)
        wo = w_out[l].astype(BF16)
        wup = w_ffn_up[l].astype(BF16)
        wdn = w_ffn_down[l].astype(BF16)
        n1w = norm1_w[l].reshape(1, d)
        n2w = norm2_w[l].reshape(1, d)
        fw = final_norm_w.reshape(1, d)
        snw = ssm_norm_w[l].reshape(1, SSM_INNER)
        cw = conv_w[l]
        cb = conv_b[l].reshape(1, CONV_DIM)
        dtb = _pad_lanes(dt_bias[l], DT_PAD)
        aneg = _pad_lanes(-jnp.exp(a_log[l].astype(F32)), DT_PAD)
        dskip = jnp.repeat(d_skip[l].astype(F32), SSM_HEAD_DIM).reshape(1, SSM_INNER)
        fcw = ffn_conv_w[l]
        fcb = ffn_conv_b[l].reshape(1, f)
        bias = sb_logit_bias[l].astype(F32)

        q, kb, vb, k32, v32, xbc, dt, xn = _inproj(hp, n1w, w_a, tm_p)
        osb = _sb_attn_prompt(q.reshape(b, n, SB_WIDTH), kb.reshape(b, n, SB_WIDTH),
                              vb.reshape(b, n, SB_WIDTH), bias).reshape(b * n, SB_WIDTH)
        y, st, cst = _ssd_prompt(xbc, dt, b, cw, cb, dtb, aneg, dskip, e_mat, et_mat)
        h1, xn2 = _mix_out(xn, hp, osb, y, wz, wg, wsb, wssm, wo, snw, n2w, tm_p)
        hp, fst = _ffn_prompt(xn2, h1, b, wup, wdn, fcw, fcb, fw, tm_f, final)
        outs["kp"].append(k32.reshape(b, n, SB_HEADS, SB_HEAD_DIM)[:, PAD:])
        outs["vp"].append(v32.reshape(b, n, SB_HEADS, SB_HEAD_DIM)[:, PAD:])
        outs["sp"].append(st.reshape(b, SSM_HEADS, SSM_HEAD_DIM, SSM_STATE))
        outs["cp"].append(cst[:, SUBLANES - (SSM_CONV - 1):])
        outs["fp"].append(fst[:, SUBLANES - (FFN_CONV - 1):])

        q, kb, vb, k32, v32, xbc, dt, xn = _inproj(hs, n1w, w_a, ns)
        osb = _sb_attn_sample(q.astype(F32) , jnp.broadcast_to(bias[:, None], (SB_HEADS, BLOCK)),
                              cache_k4, cache_v4, page_table, l).astype(BF16)
        cs_in = state_conv[l].reshape(ns, (SSM_CONV - 1) * CONV_DIM)
        y, nst, ncs = _ssd_sample(xbc, dt, cs_in, state4, l, cw, cb, dtb, aneg, dskip, e_mat, SUBLANES)
        h1, xn2 = _mix_out(xn, hs, osb, y, wz, wg, wsb, wssm, wo, snw, n2w, ns)
        fs_in = state_ffn_conv[l].reshape(ns, (FFN_CONV - 1) * f)
        hs, nfs = _ffn_sample(xn2, h1, fs_in, wup, wdn, fcw, fcb, fw, final)
        outs["ks"].append(k32.reshape(ns, 1, SB_HEADS, SB_HEAD_DIM))
        outs["vs"].append(v32.reshape(ns, 1, SB_HEADS, SB_HEAD_DIM))
        outs["ss"].append(nst.reshape(ns, SSM_HEADS, SSM_HEAD_DIM, SSM_STATE))
        outs["cs"].append(ncs.reshape(ns, SSM_CONV - 1, CONV_DIM))
        outs["fs"].append(nfs.reshape(ns, FFN_CONV - 1, f))

    y_prompt = hp.reshape(b, n, d)[:, BLOCK:]
    y_sample = hs.reshape(ns, 1, d)
    st = {k: jnp.stack(v) for k, v in outs.items()}
    return (y_prompt, y_sample, st["kp"], st["vp"], st["ks"], st["vs"],
            st["sp"], st["ss"], st["cp"], st["cs"], st["fp"], st["fs"])
```

```python
import functools

import jax
import jax.numpy as jnp
from jax import lax
from jax.experimental import pallas as pl
from jax.experimental.pallas import tpu as pltpu

F32 = jnp.float32
BF16 = jnp.bfloat16
HIGHEST = lax.Precision.HIGHEST

N_META = 16
BLOCK = 128
PAD = BLOCK - N_META
SB_HEADS = 8
SB_HEAD_DIM = 64
SB_WIDTH = SB_HEADS * SB_HEAD_DIM
SB_SCALE = SB_HEAD_DIM ** -0.5
SSM_HEADS = 16
SSM_HEAD_DIM = 64
SSM_INNER = SSM_HEADS * SSM_HEAD_DIM
SSM_GROUPS = 4
SSM_STATE = 128
SSM_CONV = 4
CONV_DIM = SSM_INNER + 2 * SSM_GROUPS * SSM_STATE
FFN_CONV = 3
EPS = 1e-6
LANES = 128
SUBLANES = 8
DT_PAD = LANES
ATTN_UNROLL = 8
VMEM_LIMIT = 56 * 1024 * 1024


def _cparams(sem):
    return pltpu.CompilerParams(dimension_semantics=sem, vmem_limit_bytes=VMEM_LIMIT)


def _const_spec(shape):
    nd = len(shape)
    return pl.BlockSpec(shape, lambda *_: (0,) * nd, pipeline_mode=pl.Buffered(1))


def _full_spec(shape):
    nd = len(shape)
    return pl.BlockSpec(shape, lambda *_: (0,) * nd)


def _bdot(a, b):
    return jnp.dot(a, b, preferred_element_type=F32)


def _dot_nt(a, b):
    return lax.dot_general(a, b, (((1,), (1,)), ((), ())), preferred_element_type=F32)


def _dot_tn(a, b, precision=None):
    return lax.dot_general(a, b, (((0,), (0,)), ((), ())), preferred_element_type=F32, precision=precision)


def _hdot(a, b):
    return jnp.dot(a, b, preferred_element_type=F32, precision=HIGHEST)


def _sigmoid(x):
    return 1.0 / (1.0 + jnp.exp(-x))


def _silu(x):
    return x * _sigmoid(x)


def _softplus(x):
    return jnp.maximum(x, 0.0) + jnp.log(1.0 + jnp.exp(-jnp.abs(x)))


def _rms(x, w):
    return x * lax.rsqrt(jnp.mean(x * x, axis=-1, keepdims=True) + EPS) * w


def _split_bf16(x):
    hi = x.astype(BF16)
    lo = (x - hi.astype(F32)).astype(BF16)
    return hi, lo


def _inproj_kernel(h_ref, nw_ref, w_ref, q_ref, kb_ref, vb_ref, k_ref, v_ref, xbc_ref, dt_ref, xn_ref):
    xn = _rms(h_ref[...], nw_ref[...]).astype(BF16)
    xn_ref[...] = xn
    qkv = _bdot(xn, w_ref[:, 0:3 * SB_WIDTH])
    q_ref[...] = (qkv[:, 0:SB_WIDTH] * SB_SCALE).astype(BF16)
    k = qkv[:, SB_WIDTH:2 * SB_WIDTH]
    v = qkv[:, 2 * SB_WIDTH:3 * SB_WIDTH]
    k_ref[...] = k
    v_ref[...] = v
    kb_ref[...] = k.astype(BF16)
    vb_ref[...] = v.astype(BF16)
    o = 3 * SB_WIDTH
    xbc_ref[...] = _bdot(xn, w_ref[:, o:o + CONV_DIM])
    dt_ref[...] = _bdot(xn, w_ref[:, o + CONV_DIM:o + CONV_DIM + DT_PAD])


def _inproj(h, norm_w, w_a, tm):
    m, d = h.shape
    wa = w_a.shape[1]
    row = lambda width: pl.BlockSpec((tm, width), lambda i: (i, 0))
    out_shape = (
        jax.ShapeDtypeStruct((m, SB_WIDTH), BF16),
        jax.ShapeDtypeStruct((m, SB_WIDTH), BF16),
        jax.ShapeDtypeStruct((m, SB_WIDTH), BF16),
        jax.ShapeDtypeStruct((m, SB_WIDTH), F32),
        jax.ShapeDtypeStruct((m, SB_WIDTH), F32),
        jax.ShapeDtypeStruct((m, CONV_DIM), F32),
        jax.ShapeDtypeStruct((m, DT_PAD), F32),
        jax.ShapeDtypeStruct((m, d), BF16),
    )
    return pl.pallas_call(
        _inproj_kernel,
        grid=(m // tm,),
        in_specs=[row(d), _const_spec((1, d)), _const_spec((d, wa))],
        out_specs=(row(SB_WIDTH), row(SB_WIDTH), row(SB_WIDTH), row(SB_WIDTH), row(SB_WIDTH),
                   row(CONV_DIM), row(DT_PAD), row(d)),
        out_shape=out_shape,
        compiler_params=_cparams(("parallel",)),
        name="inproj",
    )(h, norm_w, w_a)


def _cumsum_matrix():
    r = lax.broadcasted_iota(jnp.int32, (2 * BLOCK, 2 * BLOCK), 0) % BLOCK
    c = lax.broadcasted_iota(jnp.int32, (2 * BLOCK, 2 * BLOCK), 1)
    return jnp.where((r > c) | (c >= BLOCK), 1.0, 0.0).astype(BF16)


def _attn_kernel(bias_ref, q_ref, k_ref, v_ref, o_ref, acc_ref):
    pair = pl.program_id(1)
    qi = pl.program_id(2)
    lane = lax.broadcasted_iota(jnp.int32, (BLOCK, BLOCK), 1)
    uu = _cumsum_matrix()
    q = q_ref[...]
    zero = jnp.zeros_like(q)
    q_heads = (jnp.where(lane < SB_HEAD_DIM, q, zero), jnp.where(lane >= SB_HEAD_DIM, q, zero))
    biases = (bias_ref[2 * pair], bias_ref[2 * pair + 1])

    def process(blk, nblk, masked, carries):
        width = nblk * BLOCK
        start = blk * BLOCK if isinstance(blk, int) else pl.multiple_of(blk * BLOCK, BLOCK)
        kb = k_ref[pl.ds(start, width), :]
        vb = v_ref[pl.ds(start, width), :]
        if masked:
            t_pos = qi * BLOCK + lax.broadcasted_iota(jnp.int32, (BLOCK, width), 0)
            s_pos = start + lax.broadcasted_iota(jnp.int32, (BLOCK, width), 1)
            mask = (s_pos < t_pos) & (s_pos >= PAD)
        out = []
        for hh in range(2):
            z = _dot_nt(q_heads[hh], kb) + biases[hh]
            lk = -_softplus(z)
            if masked:
                lk = jnp.where(mask, lk, 0.0)
            hi, lo = _split_bf16(lk)
            carry = carries[hh]
            between = [None] * nblk
            for j in reversed(range(nblk)):
                sl = slice(j * BLOCK, (j + 1) * BLOCK)
                r = _bdot(jnp.concatenate([hi[:, sl], lo[:, sl]], axis=1), uu)
                between[j] = r[:, :BLOCK] + carry
                carry = carry + r[:, BLOCK:]
            w = jnp.exp(z + lk + jnp.concatenate(between, axis=1))
            if masked:
                w = jnp.where(mask, w, 0.0)
            acc_ref[hh] += _bdot(w.astype(BF16), vb)
            out.append(carry)
        return tuple(out)

    acc_ref[...] = jnp.zeros_like(acc_ref)
    zc = jnp.zeros((BLOCK, BLOCK), F32)
    carries = process(qi, 1, True, (zc, zc))
    n_inner = jnp.maximum(qi - 1, 0)
    n_full = n_inner // ATTN_UNROLL
    carries = lax.fori_loop(
        0, n_full, lambda i, c: process(qi - ATTN_UNROLL * (i + 1), ATTN_UNROLL, False, c), carries)
    top = qi - 1 - ATTN_UNROLL * n_full
    rem = n_inner - ATTN_UNROLL * n_full
    step = ATTN_UNROLL // 2
    while step >= 1:
        take = (rem & step) != 0
        carries = lax.cond(take, functools.partial(process, top - (step - 1), step, False),
                           lambda c: c, carries)
        top = top - jnp.where(take, step, 0)
        step //= 2

    @pl.when(qi > 0)
    def _():
        process(0, 1, True, carries)

    o_ref[...] = jnp.where(lane < SB_HEAD_DIM, acc_ref[0], acc_ref[1]).astype(o_ref.dtype)


def _sb_attn_prompt(q, k, v, bias):
    b, n, _ = q.shape
    nb = n // BLOCK
    return pl.pallas_call(
        _attn_kernel,
        grid=(b, SB_WIDTH // LANES, nb),
        in_specs=[
            pl.BlockSpec(memory_space=pltpu.SMEM),
            pl.BlockSpec((None, BLOCK, LANES), lambda bi, p, qi: (bi, qi, p)),
            pl.BlockSpec((None, n, LANES), lambda bi, p, qi: (bi, 0, p)),
            pl.BlockSpec((None, n, LANES), lambda bi, p, qi: (bi, 0, p)),
        ],
        out_specs=pl.BlockSpec((None, BLOCK, LANES), lambda bi, p, qi: (bi, qi, p)),
        scratch_shapes=[pltpu.VMEM((2, BLOCK, BLOCK), F32)],
        out_shape=jax.ShapeDtypeStruct((b, n, SB_WIDTH), BF16),
        compiler_params=_cparams(("parallel", "parallel", "arbitrary")),
        name="sb_attn_prompt",
    )(bias, q, k, v)


def _ssd_kernel(xbc_ref, dt_ref, cw_ref, cb_ref, dtb_ref, aneg_ref, dskip_ref, e_ref, et_ref,
                y_ref, st_ref, cs_ref, hist_ref):
    c = pl.program_id(1)
    row = lax.broadcasted_iota(jnp.int32, (BLOCK, 1), 0)
    valid = jnp.where((c > 0) | (row >= PAD), 1.0, 0.0)

    @pl.when(c == 0)
    def _():
        hist_ref[0:SUBLANES, :] = jnp.zeros((SUBLANES, CONV_DIM), F32)
        st_ref[...] = jnp.zeros_like(st_ref)

    x = xbc_ref[...] * valid
    hist_ref[SUBLANES:SUBLANES + BLOCK, :] = x
    conv = cb_ref[...] + x * cw_ref[SSM_CONV - 1:SSM_CONV, :]
    for j in range(SSM_CONV - 1):
        back = SSM_CONV - 1 - j
        conv = conv + hist_ref[SUBLANES - back:SUBLANES - back + BLOCK, :] * cw_ref[j:j + 1, :]
    tail = x[BLOCK - SUBLANES:, :]
    hist_ref[0:SUBLANES, :] = tail
    cs_ref[...] = tail
    xc = _silu(conv)
    xs = xc[:, :SSM_INNER]
    bm = xc[:, SSM_INNER:SSM_INNER + SSM_GROUPS * SSM_STATE].astype(BF16)
    cm = xc[:, SSM_INNER + SSM_GROUPS * SSM_STATE:].astype(BF16)

    dt = _softplus(dt_ref[...] + dtb_ref[...]) * valid
    la = dt * aneg_ref[...]
    ri = lax.broadcasted_iota(jnp.int32, (BLOCK, BLOCK), 0)
    ci = lax.broadcasted_iota(jnp.int32, (BLOCK, BLOCK), 1)
    causal = ci <= ri
    tri = jnp.where(causal, 1.0, 0.0)
    cum = _hdot(tri, la)
    cum_t = cum.T
    e = e_ref[...]
    xdt = xs * _hdot(dt, e)
    ecum = _hdot(jnp.exp(cum), e)
    cum_last = cum[BLOCK - 1:BLOCK, :]
    xdtw = (xdt * _hdot(jnp.exp(cum_last - cum), e)).astype(BF16)
    xdt_b = xdt.astype(BF16)
    ecl = jnp.exp(cum_t[:, BLOCK - 1:BLOCK])
    fac = _hdot(et_ref[...], jnp.broadcast_to(ecl, (DT_PAD, SSM_STATE)))
    lane = ci

    for g in range(SSM_GROUPS):
        cmg = cm[:, g * SSM_STATE:(g + 1) * SSM_STATE]
        bmg = bm[:, g * SSM_STATE:(g + 1) * SSM_STATE]
        cb = _dot_nt(cmg, bmg)
        for pp in range(2):
            pair = 2 * g + pp
            cols = slice(pair * LANES, (pair + 1) * LANES)
            ys = []
            for hh in range(2):
                h = 2 * pair + hh
                seg = cum[:, h:h + 1] - cum_t[h:h + 1, :]
                m = (cb * jnp.where(causal, jnp.exp(seg), 0.0)).astype(BF16)
                ys.append(_bdot(m, xdt_b[:, cols]))
            y = jnp.where(lane < SSM_HEAD_DIM, ys[0], ys[1])
            st = st_ref[cols, :]
            y = y + _dot_nt(cmg, st.astype(BF16)) * ecum[:, cols]
            st_ref[cols, :] = st * fac[cols, :] + _dot_tn(xdtw[:, cols], bmg)
            y_ref[:, cols] = y + xs[:, cols] * dskip_ref[:, cols]


def _ssd_prompt(xbc, dt, b, cw, cb, dtb, aneg, dskip, e, et):
    m = xbc.shape[0]
    nc = m // b // BLOCK
    row = lambda width: pl.BlockSpec((BLOCK, width), lambda bi, c: (bi * nc + c, 0))
    out_shape = (
        jax.ShapeDtypeStruct((m, SSM_INNER), F32),
        jax.ShapeDtypeStruct((b, SSM_INNER, SSM_STATE), F32),
        jax.ShapeDtypeStruct((b, SUBLANES, CONV_DIM), F32),
    )
    return pl.pallas_call(
        _ssd_kernel,
        grid=(b, nc),
        in_specs=[row(CONV_DIM), row(DT_PAD), _const_spec(cw.shape), _const_spec(cb.shape),
                  _const_spec(dtb.shape), _const_spec(aneg.shape), _const_spec(dskip.shape),
                  _const_spec(e.shape), _const_spec(et.shape)],
        out_specs=(row(SSM_INNER),
                   pl.BlockSpec((None, SSM_INNER, SSM_STATE), lambda bi, c: (bi, 0, 0)),
                   pl.BlockSpec((None, SUBLANES, CONV_DIM), lambda bi, c: (bi, 0, 0))),
        out_shape=out_shape,
        scratch_shapes=[pltpu.VMEM((SUBLANES + BLOCK, CONV_DIM), F32)],
        compiler_params=_cparams(("parallel", "arbitrary")),
        name="ssd_prompt",
    )(xbc, dt, cw, cb, dtb, aneg, dskip, e, et)


def _mix_kernel(xn_ref, h_ref, osb_ref, y_ref, wz_ref, wg_ref, wsb_ref, wssm_ref, wo_ref, snw_ref, n2w_ref,
                h1_ref, xn2_ref):
    xn = xn_ref[...]
    d = h_ref.shape[1]
    z = _bdot(xn, wz_ref[...])
    yz = y_ref[...] * _silu(z)
    gw = SSM_INNER // SSM_GROUPS
    parts = []
    for g in range(SSM_GROUPS):
        yg = yz[:, g * gw:(g + 1) * gw]
        parts.append(yg * lax.rsqrt(jnp.mean(yg * yg, axis=-1, keepdims=True) + EPS))
    yn = (jnp.concatenate(parts, axis=1) * snw_ref[...]).astype(BF16)
    g_a = _sigmoid(_bdot(xn, wg_ref[:, 0:d]))
    merged = g_a * _bdot(osb_ref[...], wsb_ref[...])
    g_s = _sigmoid(_bdot(xn, wg_ref[:, d:2 * d]))
    merged = merged + g_s * _bdot(yn, wssm_ref[...])
    h1 = h_ref[...] + _bdot(merged.astype(BF16), wo_ref[...])
    h1_ref[...] = h1
    xn2_ref[...] = _rms(h1, n2w_ref[...]).astype(BF16)


def _mix_out(xn, h, osb, y, wz, wg, wsb, wssm, wo, snw, n2w, tm):
    m, d = h.shape
    row = lambda width: pl.BlockSpec((tm, width), lambda i: (i, 0))
    return pl.pallas_call(
        _mix_kernel,
        grid=(m // tm,),
        in_specs=[row(d), row(d), row(SB_WIDTH), row(SSM_INNER)] +
                 [_const_spec(a.shape) for a in (wz, wg, wsb, wssm, wo, snw, n2w)],
        out_specs=(row(d), row(d)),
        out_shape=(jax.ShapeDtypeStruct((m, d), F32), jax.ShapeDtypeStruct((m, d), BF16)),
        compiler_params=_cparams(("parallel",)),
        name="mix_out",
    )(xn, h, osb, y, wz, wg, wsb, wssm, wo, snw, n2w)


def _ffn_tail(hdn, h1, wdn_ref, fw_ref, out_ref, final):
    h2 = h1 + _bdot(hdn.astype(BF16), wdn_ref[...])
    out_ref[...] = _rms(h2, fw_ref[...]) if final else h2


def _ffn_prompt_kernel(xn2_ref, h1_ref, wup_ref, wdn_ref, cw_ref, cb_ref, fw_ref, out_ref, fs_ref, hist_ref,
                       *, final):
    t = pl.program_id(1)
    tm = xn2_ref.shape[0]
    f = cw_ref.shape[1]

    @pl.when(t == 0)
    def _():
        hist_ref[0:SUBLANES, :] = jnp.zeros((SUBLANES, f), F32)

    row = lax.broadcasted_iota(jnp.int32, (tm, 1), 0)
    valid = jnp.where((t > 0) | (row >= PAD), 1.0, 0.0)
    xn2 = xn2_ref[...]
    gate = _bdot(xn2, wup_ref[:, 0:f]) * valid
    hist_ref[SUBLANES:SUBLANES + tm, :] = gate
    conv = cb_ref[...] + gate * cw_ref[FFN_CONV - 1:FFN_CONV, :]
    for j in range(FFN_CONV - 1):
        back = FFN_CONV - 1 - j
        conv = conv + hist_ref[SUBLANES - back:SUBLANES - back + tm, :] * cw_ref[j:j + 1, :]
    tail = gate[tm - SUBLANES:, :]
    hist_ref[0:SUBLANES, :] = tail
    fs_ref[...] = tail
    hdn = _silu(conv) * _bdot(xn2, wup_ref[:, f:2 * f])
    _ffn_tail(hdn, h1_ref[...], wdn_ref, fw_ref, out_ref, final)


def _ffn_prompt(xn2, h1, b, wup, wdn, cw, cb, fw, tm, final):
    m, d = h1.shape
    f = cw.shape[1]
    nt = m // b // tm
    row = lambda width: pl.BlockSpec((tm, width), lambda bi, t: (bi * nt + t, 0))
    return pl.pallas_call(
        functools.partial(_ffn_prompt_kernel, final=final),
        grid=(b, nt),
        in_specs=[row(d), row(d)] + [_const_spec(a.shape) for a in (wup, wdn, cw, cb, fw)],
        out_specs=(row(d), pl.BlockSpec((None, SUBLANES, f), lambda bi, t: (bi, 0, 0))),
        out_shape=(jax.ShapeDtypeStruct((m, d), F32), jax.ShapeDtypeStruct((b, SUBLANES, f), F32)),
        scratch_shapes=[pltpu.VMEM((SUBLANES + tm, f), F32)],
        compiler_params=_cparams(("parallel", "arbitrary")),
        name="ffn_prompt",
    )(xn2, h1, wup, wdn, cw, cb, fw)


def _ffn_sample_kernel(xn2_ref, h1_ref, hist_ref, wup_ref, wdn_ref, cw_ref, cb_ref, fw_ref, out_ref, fs_ref,
                       *, final):
    f = cw_ref.shape[1]
    xn2 = xn2_ref[...]
    gate = _bdot(xn2, wup_ref[:, 0:f])
    conv = cb_ref[...] + gate * cw_ref[FFN_CONV - 1:FFN_CONV, :]
    for j in range(FFN_CONV - 1):
        conv = conv + hist_ref[:, j * f:(j + 1) * f] * cw_ref[j:j + 1, :]
    for j in range(FFN_CONV - 2):
        fs_ref[:, j * f:(j + 1) * f] = hist_ref[:, (j + 1) * f:(j + 2) * f]
    fs_ref[:, (FFN_CONV - 2) * f:(FFN_CONV - 1) * f] = gate
    hdn = _silu(conv) * _bdot(xn2, wup_ref[:, f:2 * f])
    _ffn_tail(hdn, h1_ref[...], wdn_ref, fw_ref, out_ref, final)


def _ffn_sample(xn2, h1, hist, wup, wdn, cw, cb, fw, final):
    m, d = h1.shape
    return pl.pallas_call(
        functools.partial(_ffn_sample_kernel, final=final),
        grid=(1,),
        in_specs=[_const_spec(a.shape) for a in (xn2, h1, hist, wup, wdn, cw, cb, fw)],
        out_specs=(_full_spec((m, d)), _full_spec(hist.shape)),
        out_shape=(jax.ShapeDtypeStruct((m, d), F32), jax.ShapeDtypeStruct(hist.shape, F32)),
        compiler_params=_cparams(("arbitrary",)),
        name="ffn_sample",
    )(xn2, h1, hist, wup, wdn, cw, cb, fw)


def _attn_sample_kernel(pt_ref, q_ref, bias_ref, um_ref, *refs, n_pages):
    k_refs = refs[:n_pages]
    v_refs = refs[n_pages:2 * n_pages]
    o_ref = refs[2 * n_pages]
    page_w = BLOCK * SB_HEADS
    q8 = q_ref[...].astype(BF16)
    own = (lax.broadcasted_iota(jnp.int32, (SB_HEADS, page_w), 1) % SB_HEADS ==
           lax.broadcasted_iota(jnp.int32, (SB_HEADS, page_w), 0))
    prow = lax.broadcasted_iota(jnp.int32, (n_pages, page_w), 0)
    z = jnp.zeros((n_pages, page_w), F32)
    for c in range(n_pages):
        kf = k_refs[c][...].reshape(page_w, SB_HEAD_DIM).astype(BF16)
        zc = jnp.sum(jnp.where(own, _dot_nt(q8, kf), 0.0), axis=0, keepdims=True)
        z = jnp.where(prow == c, jnp.broadcast_to(zc, (n_pages, page_w)), z)
    z = z + bias_ref[...]
    lk = -_softplus(z)
    hi, lo = _split_bf16(lk)
    r = _bdot(jnp.concatenate([hi, lo], axis=0), um_ref[...])
    r = r[:n_pages] + r[n_pages:]
    within = r[:, :page_w]
    thi, tlo = _split_bf16(r[:, page_w:])
    pa = lax.broadcasted_iota(jnp.int32, (n_pages, n_pages), 0)
    pb = lax.broadcasted_iota(jnp.int32, (n_pages, n_pages), 1)
    mx = jnp.where(pb > pa, 1.0, 0.0).astype(BF16)
    lt = _bdot(mx, jnp.concatenate([thi, tlo], axis=1))
    later = lt[:, :page_w] + lt[:, page_w:]
    w = jnp.exp(z + lk + within + later)
    o = jnp.zeros((SB_HEADS, SB_HEAD_DIM), F32)
    for c in range(n_pages):
        w8 = jnp.where(own, jnp.broadcast_to(w[c:c + 1, :], (SB_HEADS, page_w)), 0.0).astype(BF16)
        vf = v_refs[c][...].reshape(page_w, SB_HEAD_DIM).astype(BF16)
        o = o + _bdot(w8, vf)
    o_ref[...] = o


def _sb_attn_sample(q, bias, cache_k, cache_v, page_table, layer):
    s = q.shape[0]
    n_pages = page_table.shape[1]
    page = cache_k.shape[2]
    assert page == BLOCK
    page_w = BLOCK * SB_HEADS
    bias_l = jnp.tile(bias, BLOCK).reshape(1, page_w)
    j = jnp.arange(page_w, dtype=jnp.int32)
    same_head = (j[:, None] % SB_HEADS) == (j[None, :] % SB_HEADS)
    later_key = (j[:, None] // SB_HEADS) > (j[None, :] // SB_HEADS)
    um = jnp.concatenate([same_head & later_key, same_head], axis=1).astype(BF16)

    def page_spec(c):
        return pl.BlockSpec((None, None, page, SB_HEADS, SB_HEAD_DIM),
                            lambda i, pt, c=c: (layer, pt[i * n_pages + c], 0, 0, 0))

    grid_spec = pltpu.PrefetchScalarGridSpec(
        num_scalar_prefetch=1,
        grid=(s,),
        in_specs=[pl.BlockSpec((None, SB_HEADS, SB_HEAD_DIM), lambda i, pt: (i, 0, 0)),
                  pl.BlockSpec((1, page_w), lambda i, pt: (0, 0)),
                  pl.BlockSpec((page_w, 2 * page_w), lambda i, pt: (0, 0), pipeline_mode=pl.Buffered(1))] +
                 [page_spec(c) for c in range(n_pages)] * 2,
        out_specs=pl.BlockSpec((None, SB_HEADS, SB_HEAD_DIM), lambda i, pt: (i, 0, 0)),
    )
    return pl.pallas_call(
        functools.partial(_attn_sample_kernel, n_pages=n_pages),
        grid_spec=grid_spec,
        out_shape=jax.ShapeDtypeStruct((s, SB_HEADS, SB_HEAD_DIM), F32),
        compiler_params=_cparams(("arbitrary",)),
        name="sb_attn_sample",
    )(page_table.reshape(-1), q, bias_l, um, *([cache_k] * n_pages), *([cache_v] * n_pages))


def _ssd_sample_kernel(xbc_ref, dt_ref, cs_ref, st_ref, cw_ref, cb_ref, dtb_ref, aneg_ref, dskip_ref, e_ref,
                       y_ref, nst_ref, ncs_ref):
    sb = xbc_ref.shape[0]
    x = xbc_ref[...]
    conv = cb_ref[...] + x * cw_ref[SSM_CONV - 1:SSM_CONV, :]
    for j in range(SSM_CONV - 1):
        conv = conv + cs_ref[:, j * CONV_DIM:(j + 1) * CONV_DIM] * cw_ref[j:j + 1, :]
    for j in range(SSM_CONV - 2):
        ncs_ref[:, j * CONV_DIM:(j + 1) * CONV_DIM] = cs_ref[:, (j + 1) * CONV_DIM:(j + 2) * CONV_DIM]
    ncs_ref[:, (SSM_CONV - 2) * CONV_DIM:(SSM_CONV - 1) * CONV_DIM] = x
    xc = _silu(conv)
    xs = xc[:, :SSM_INNER]
    bm = xc[:, SSM_INNER:SSM_INNER + SSM_GROUPS * SSM_STATE]
    cm = xc[:, SSM_INNER + SSM_GROUPS * SSM_STATE:]
    dt = _softplus(dt_ref[...] + dtb_ref[...])
    e = e_ref[...]
    xdt = xs * _hdot(dt, e)
    dec = _hdot(jnp.exp(dt * aneg_ref[...]), e)
    gw = SSM_INNER // SSM_GROUPS
    cb_parts = []
    for g in range(SSM_GROUPS):
        sl = slice(g * SSM_STATE, (g + 1) * SSM_STATE)
        cbg = jnp.sum(cm[:, sl] * bm[:, sl], axis=-1, keepdims=True)
        cb_parts.append(jnp.broadcast_to(cbg, (sb, gw)))
    y = jnp.concatenate(cb_parts, axis=1) * xdt + xs * dskip_ref[...]

    grow = lax.broadcasted_iota(jnp.int32, (SUBLANES, SSM_INNER), 0)
    gcol = lax.broadcasted_iota(jnp.int32, (SUBLANES, SSM_INNER), 1) // gw
    gmask = grow == gcol
    r8 = lax.broadcasted_iota(jnp.int32, (SUBLANES, SSM_STATE), 0)
    ones0 = jnp.where(r8 == 0, 1.0, 0.0)
    row8 = lax.broadcasted_iota(jnp.int32, (SUBLANES, SSM_INNER), 0)
    rsb = lax.broadcasted_iota(jnp.int32, (sb, SSM_INNER), 0)
    ystate = jnp.zeros((sb, SSM_INNER), F32)

    def group_rows(v, i):
        out = jnp.zeros((SUBLANES, SSM_STATE), F32)
        for g in range(SSM_GROUPS):
            vg = jnp.broadcast_to(v[i:i + 1, g * SSM_STATE:(g + 1) * SSM_STATE], (SUBLANES, SSM_STATE))
            out = jnp.where(r8 == g, vg, out)
        return out

    for i in range(sb):
        st = st_ref[i]
        y8 = _dot_nt(group_rows(cm, i).astype(BF16), st.astype(BF16))
        ys = jnp.sum(jnp.where(gmask, y8, 0.0), axis=0, keepdims=True)
        ystate = jnp.where(rsb == i, jnp.broadcast_to(ys, (sb, SSM_INNER)), ystate)
        xg = jnp.where(gmask, jnp.broadcast_to(xdt[i:i + 1, :], (SUBLANES, SSM_INNER)), 0.0)
        d8 = jnp.where(row8 == 0, jnp.broadcast_to(dec[i:i + 1, :], (SUBLANES, SSM_INNER)), 0.0)
        fac = _dot_tn(d8, ones0, precision=HIGHEST)
        inc = _dot_tn(xg, group_rows(bm, i), precision=HIGHEST)
        nst_ref[i] = st * fac + inc
    y_ref[...] = y + ystate * dec


def _ssd_sample(xbc, dt, cs, state, layer, cw, cb, dtb, aneg, dskip, e, sb):
    s = xbc.shape[0]
    row = lambda width: pl.BlockSpec((sb, width), lambda i: (i, 0))
    cs_w = cs.shape[1]
    return pl.pallas_call(
        _ssd_sample_kernel,
        grid=(s // sb,),
        in_specs=[row(CONV_DIM), row(DT_PAD), row(cs_w),
                  pl.BlockSpec((None, sb, SSM_INNER, SSM_STATE), lambda i: (layer, i, 0, 0))] +
                 [_const_spec(a.shape) for a in (cw, cb, dtb, aneg, dskip, e)],
        out_specs=(row(SSM_INNER),
                   pl.BlockSpec((sb, SSM_INNER, SSM_STATE), lambda i: (i, 0, 0)),
                   row(cs_w)),
        out_shape=(jax.ShapeDtypeStruct((s, SSM_INNER), F32),
                   jax.ShapeDtypeStruct((s, SSM_INNER, SSM_STATE), F32),
                   jax.ShapeDtypeStruct((s, cs_w), F32)),
        compiler_params=_cparams(("parallel",)),
        name="ssd_sample",
    )(xbc, dt, cs, state, cw, cb, dtb, aneg, dskip, e)


def _pad_lanes(v, width):
    return jnp.pad(v.astype(F32), (0, width - v.shape[0])).reshape(1, width)


def kernel(x_prompt, x_sample, cache_k, cache_v, state_ssm, state_conv, state_ffn_conv, page_table,
           meta_tokens, norm1_w, w_in, sb_logit_bias, conv_w, conv_b, dt_bias, a_log, d_skip, ssm_norm_w,
           w_sb_out, w_ssm_out, w_out, norm2_w, w_ffn_up, ffn_conv_w, ffn_conv_b, w_ffn_down, final_norm_w):
    b, s, d = x_prompt.shape
    depth = w_in.shape[0]
    n = s + BLOCK
    ns = x_sample.shape[0]
    f = ffn_conv_w.shape[2]
    assert x_sample.shape[1] == 1 and n % BLOCK == 0
    tm_p = 512
    tm_f = n // 8
    assert (b * n) % tm_p == 0 and tm_f % SUBLANES == 0

    hp = jnp.concatenate([jnp.zeros((b, PAD, d), F32),
                          jnp.broadcast_to(meta_tokens.astype(F32)[None], (b, N_META, d)),
                          x_prompt], axis=1).reshape(b * n, d)
    hs = x_sample.reshape(ns, d)

    hid = jnp.arange(SSM_INNER, dtype=jnp.int32) // SSM_HEAD_DIM
    e_mat = (jnp.arange(DT_PAD, dtype=jnp.int32)[:, None] == hid[None, :]).astype(F32)
    et_mat = e_mat.T
    state4 = state_ssm.reshape(depth, ns, SSM_INNER, SSM_STATE)

    o_qkv = 3 * SB_WIDTH
    o_z = o_qkv
    o_xbc = o_z + SSM_INNER
    o_dt = o_xbc + CONV_DIM
    o_g = o_dt + SSM_HEADS

    outs = {k: [] for k in ("kp", "vp", "sp", "cp", "fp", "ks", "vs", "ss", "cs", "fs")}
    for l in range(depth):
        final = l == depth - 1
        wl = w_in[l]
        w_a = jnp.concatenate([wl[:, :o_qkv], wl[:, o_xbc:o_dt],
                               jnp.pad(wl[:, o_dt:o_g], ((0, 0), (0, DT_PAD - SSM_HEADS)))], axis=1).astype(BF16)
        wz = wl[:, o_z:o_xbc].astype(BF16)
        wg = wl[:, o_g:].astype(BF16)
        wsb = w_sb_out[l].astype(BF16)
        wssm = w_ssm_out[l].astype(BF16)
        wo = w_out[l].astype(BF16)
        wup = w_ffn_up[l].astype(BF16)
        wdn = w_ffn_down[l].astype(BF16)
        n1w = norm1_w[l].reshape(1, d)
        n2w = norm2_w[l].reshape(1, d)
        fw = final_norm_w.reshape(1, d)
        snw = ssm_norm_w[l].reshape(1, SSM_INNER)
        cw = conv_w[l]
        cb = conv_b[l].reshape(1, CONV_DIM)
        dtb = _pad_lanes(dt_bias[l], DT_PAD)
        aneg = _pad_lanes(-jnp.exp(a_log[l].astype(F32)), DT_PAD)
        dskip = jnp.repeat(d_skip[l].astype(F32), SSM_HEAD_DIM).reshape(1, SSM_INNER)
        fcw = ffn_conv_w[l]
        fcb = ffn_conv_b[l].reshape(1, f)
        bias = sb_logit_bias[l].astype(F32)

        q, kb, vb, k32, v32, xbc, dt, xn = _inproj(hp, n1w, w_a, tm_p)
        osb = _sb_attn_prompt(q.reshape(b, n, SB_WIDTH), kb.reshape(b, n, SB_WIDTH),
                              vb.reshape(b, n, SB_WIDTH), bias).reshape(b * n, SB_WIDTH)
        y, st, cst = _ssd_prompt(xbc, dt, b, cw, cb, dtb, aneg, dskip, e_mat, et_mat)
        h1, xn2 = _mix_out(xn, hp, osb, y, wz, wg, wsb, wssm, wo, snw, n2w, tm_p)
        hp, fst = _ffn_prompt(xn2, h1, b, wup, wdn, fcw, fcb, fw, tm_f, final)
        outs["kp"].append(k32.reshape(b, n, SB_HEADS, SB_HEAD_DIM)[:, PAD:])
        outs["vp"].append(v32.reshape(b, n, SB_HEADS, SB_HEAD_DIM)[:, PAD:])
        outs["sp"].append(st.reshape(b, SSM_HEADS, SSM_HEAD_DIM, SSM_STATE))
        outs["cp"].append(cst[:, SUBLANES - (SSM_CONV - 1):])
        outs["fp"].append(fst[:, SUBLANES - (FFN_CONV - 1):])

        q, kb, vb, k32, v32, xbc, dt, xn = _inproj(hs, n1w, w_a, ns)
        osb = _sb_attn_sample(q.astype(F32).reshape(ns, SB_HEADS, SB_HEAD_DIM), bias,
                              cache_k, cache_v, page_table, l).reshape(ns, SB_WIDTH).astype(BF16)
        cs_in = state_conv[l].reshape(ns, (SSM_CONV - 1) * CONV_DIM)
        y, nst, ncs = _ssd_sample(xbc, dt, cs_in, state4, l, cw, cb, dtb, aneg, dskip, e_mat, SUBLANES)
        h1, xn2 = _mix_out(xn, hs, osb, y, wz, wg, wsb, wssm, wo, snw, n2w, ns)
        fs_in = state_ffn_conv[l].reshape(ns, (FFN_CONV - 1) * f)
        hs, nfs = _ffn_sample(xn2, h1, fs_in, wup, wdn, fcw, fcb, fw, final)
        outs["ks"].append(k32.reshape(ns, 1, SB_HEADS, SB_HEAD_DIM))
        outs["vs"].append(v32.reshape(ns, 1, SB_HEADS, SB_HEAD_DIM))
        outs["ss"].append(nst.reshape(ns, SSM_HEADS, SSM_HEAD_DIM, SSM_STATE))
        outs["cs"].append(ncs.reshape(ns, SSM_CONV - 1, CONV_DIM))
        outs["fs"].append(nfs.reshape(ns, FFN_CONV - 1, f))

    y_prompt = hp.reshape(b, n, d)[:, BLOCK:]
    y_sample = hs.reshape(ns, 1, d)
    st = {k: jnp.stack(v) for k, v in outs.items()}
    return (y_prompt, y_sample, st["kp"], st["vp"], st["ks"], st["vs"],
            st["sp"], st["ss"], st["cp"], st["cs"], st["fp"], st["fs"])
```

```python
import functools

import jax
import jax.numpy as jnp
from jax import lax
from jax.experimental import pallas as pl
from jax.experimental.pallas import tpu as pltpu

F32 = jnp.float32
BF16 = jnp.bfloat16
HIGHEST = lax.Precision.HIGHEST

N_META = 16
BLOCK = 128
PAD = BLOCK - N_META
SB_HEADS = 8
SB_HEAD_DIM = 64
SB_WIDTH = SB_HEADS * SB_HEAD_DIM
SB_SCALE = SB_HEAD_DIM ** -0.5
SSM_HEADS = 16
SSM_HEAD_DIM = 64
SSM_INNER = SSM_HEADS * SSM_HEAD_DIM
SSM_GROUPS = 4
SSM_STATE = 128
SSM_CONV = 4
CONV_DIM = SSM_INNER + 2 * SSM_GROUPS * SSM_STATE
FFN_CONV = 3
EPS = 1e-6
LANES = 128
SUBLANES = 8
DT_PAD = LANES
ATTN_HEADS = 8
ATTN_UNROLL = 4
VMEM_LIMIT = 56 * 1024 * 1024


def _cparams(sem):
    return pltpu.CompilerParams(dimension_semantics=sem, vmem_limit_bytes=VMEM_LIMIT)


def _const_spec(shape):
    nd = len(shape)
    return pl.BlockSpec(shape, lambda *_: (0,) * nd, pipeline_mode=pl.Buffered(1))


def _full_spec(shape):
    nd = len(shape)
    return pl.BlockSpec(shape, lambda *_: (0,) * nd)


def _bdot(a, b):
    return jnp.dot(a, b, preferred_element_type=F32)


def _dot_nt(a, b):
    return lax.dot_general(a, b, (((1,), (1,)), ((), ())), preferred_element_type=F32)


def _dot_tn(a, b, precision=None):
    return lax.dot_general(a, b, (((0,), (0,)), ((), ())), preferred_element_type=F32, precision=precision)


def _hdot(a, b):
    return jnp.dot(a, b, preferred_element_type=F32, precision=HIGHEST)


def _sigmoid(x):
    return 1.0 / (1.0 + jnp.exp(-x))


def _silu(x):
    return x * _sigmoid(x)


def _softplus(x):
    return jnp.maximum(x, 0.0) + jnp.log(1.0 + jnp.exp(-jnp.abs(x)))


def _rms(x, w):
    return x * lax.rsqrt(jnp.mean(x * x, axis=-1, keepdims=True) + EPS) * w


def _split_bf16(x):
    hi = x.astype(BF16)
    lo = (x - hi.astype(F32)).astype(BF16)
    return hi, lo


def _inproj_kernel(h_ref, nw_ref, w_ref, q_ref, kb_ref, vb_ref, k_ref, v_ref, xbc_ref, dt_ref, xn_ref):
    xn = _rms(h_ref[...], nw_ref[...]).astype(BF16)
    xn_ref[...] = xn
    qkv = _bdot(xn, w_ref[:, 0:3 * SB_WIDTH])
    q_ref[...] = (qkv[:, 0:SB_WIDTH] * SB_SCALE).astype(BF16)
    k = qkv[:, SB_WIDTH:2 * SB_WIDTH]
    v = qkv[:, 2 * SB_WIDTH:3 * SB_WIDTH]
    k_ref[...] = k
    v_ref[...] = v
    kb_ref[...] = k.astype(BF16)
    vb_ref[...] = v.astype(BF16)
    o = 3 * SB_WIDTH
    xbc_ref[...] = _bdot(xn, w_ref[:, o:o + CONV_DIM])
    dt_ref[...] = _bdot(xn, w_ref[:, o + CONV_DIM:o + CONV_DIM + DT_PAD])


def _inproj(h, norm_w, w_a, tm):
    m, d = h.shape
    wa = w_a.shape[1]
    row = lambda width: pl.BlockSpec((tm, width), lambda i: (i, 0))
    out_shape = (
        jax.ShapeDtypeStruct((m, SB_WIDTH), BF16),
        jax.ShapeDtypeStruct((m, SB_WIDTH), BF16),
        jax.ShapeDtypeStruct((m, SB_WIDTH), BF16),
        jax.ShapeDtypeStruct((m, SB_WIDTH), F32),
        jax.ShapeDtypeStruct((m, SB_WIDTH), F32),
        jax.ShapeDtypeStruct((m, CONV_DIM), F32),
        jax.ShapeDtypeStruct((m, DT_PAD), F32),
        jax.ShapeDtypeStruct((m, d), BF16),
    )
    return pl.pallas_call(
        _inproj_kernel,
        grid=(m // tm,),
        in_specs=[row(d), _const_spec((1, d)), _const_spec((d, wa))],
        out_specs=(row(SB_WIDTH), row(SB_WIDTH), row(SB_WIDTH), row(SB_WIDTH), row(SB_WIDTH),
                   row(CONV_DIM), row(DT_PAD), row(d)),
        out_shape=out_shape,
        compiler_params=_cparams(("parallel",)),
        name="inproj",
    )(h, norm_w, w_a)


def _cumsum_matrix():
    r = lax.broadcasted_iota(jnp.int32, (2 * BLOCK, 2 * BLOCK), 0) % BLOCK
    c = lax.broadcasted_iota(jnp.int32, (2 * BLOCK, 2 * BLOCK), 1)
    return jnp.where((r > c) | (c >= BLOCK), 1.0, 0.0).astype(BF16)


def _attn_kernel(bias_ref, q_ref, k_ref, v_ref, o_ref, acc_ref, carry_ref):
    hg = pl.program_id(1)
    qi = pl.program_id(2)
    lane = lax.broadcasted_iota(jnp.int32, (BLOCK, BLOCK), 1)
    uu = _cumsum_matrix()
    q_heads, biases = [], []
    for h in range(ATTN_HEADS):
        q = q_ref[:, (h // 2) * LANES:(h // 2 + 1) * LANES]
        keep = (lane < SB_HEAD_DIM) if h % 2 == 0 else (lane >= SB_HEAD_DIM)
        q_heads.append(jnp.where(keep, q, jnp.zeros_like(q)))
        biases.append(bias_ref[hg * ATTN_HEADS + h])

    def process(blk, nblk, masked):
        width = nblk * BLOCK
        start = blk * BLOCK if isinstance(blk, int) else pl.multiple_of(blk * BLOCK, BLOCK)
        if masked:
            t_pos = qi * BLOCK + lax.broadcasted_iota(jnp.int32, (BLOCK, width), 0)
            s_pos = start + lax.broadcasted_iota(jnp.int32, (BLOCK, width), 1)
            mask = (s_pos < t_pos) & (s_pos >= PAD)
        for h in range(ATTN_HEADS):
            cols = slice((h // 2) * LANES, (h // 2 + 1) * LANES)
            z = _dot_nt(q_heads[h], k_ref[pl.ds(start, width), cols]) + biases[h]
            lk = -_softplus(z)
            if masked:
                lk = jnp.where(mask, lk, 0.0)
            hi, lo = _split_bf16(lk)
            carry = carry_ref[h]
            between = [None] * nblk
            for j in reversed(range(nblk)):
                sl = slice(j * BLOCK, (j + 1) * BLOCK)
                r = _bdot(jnp.concatenate([hi[:, sl], lo[:, sl]], axis=1), uu)
                between[j] = r[:, :BLOCK] + carry
                carry = carry + r[:, BLOCK:]
            carry_ref[h] = carry
            w = jnp.exp(z + lk + jnp.concatenate(between, axis=1))
            if masked:
                w = jnp.where(mask, w, 0.0)
            acc_ref[h] += _bdot(w.astype(BF16), v_ref[pl.ds(start, width), cols])

    acc_ref[...] = jnp.zeros_like(acc_ref)
    carry_ref[...] = jnp.zeros_like(carry_ref)
    process(qi, 1, True)
    n_inner = jnp.maximum(qi - 1, 0)
    n_full = n_inner // ATTN_UNROLL

    def body(i, _):
        process(qi - ATTN_UNROLL * (i + 1), ATTN_UNROLL, False)
        return 0

    lax.fori_loop(0, n_full, body, 0)
    top = qi - 1 - ATTN_UNROLL * n_full
    rem = n_inner - ATTN_UNROLL * n_full
    step = ATTN_UNROLL // 2
    while step >= 1:
        take = (rem & step) != 0
        pl.when(take)(functools.partial(process, top - (step - 1), step, False))
        top = top - jnp.where(take, step, 0)
        step //= 2

    pl.when(qi > 0)(functools.partial(process, 0, 1, True))

    for p in range(ATTN_HEADS // 2):
        o_ref[:, p * LANES:(p + 1) * LANES] = jnp.where(
            lane < SB_HEAD_DIM, acc_ref[2 * p], acc_ref[2 * p + 1]).astype(o_ref.dtype)


def _sb_attn_prompt(q, k, v, bias):
    b, n, _ = q.shape
    nb = n // BLOCK
    gw = ATTN_HEADS * SB_HEAD_DIM
    return pl.pallas_call(
        _attn_kernel,
        grid=(b, SB_WIDTH // gw, nb),
        in_specs=[
            pl.BlockSpec(memory_space=pltpu.SMEM),
            pl.BlockSpec((None, BLOCK, gw), lambda bi, p, qi: (bi, qi, p)),
            pl.BlockSpec((None, n, gw), lambda bi, p, qi: (bi, 0, p)),
            pl.BlockSpec((None, n, gw), lambda bi, p, qi: (bi, 0, p)),
        ],
        out_specs=pl.BlockSpec((None, BLOCK, gw), lambda bi, p, qi: (bi, qi, p)),
        scratch_shapes=[pltpu.VMEM((ATTN_HEADS, BLOCK, BLOCK), F32), pltpu.VMEM((ATTN_HEADS, BLOCK, BLOCK), F32)],
        out_shape=jax.ShapeDtypeStruct((b, n, SB_WIDTH), BF16),
        compiler_params=_cparams(("parallel", "parallel", "arbitrary")),
        name="sb_attn_prompt",
    )(bias, q, k, v)


def _ssd_kernel(xbc_ref, dt_ref, cw_ref, cb_ref, dtb_ref, aneg_ref, dskip_ref, e_ref, et_ref,
                y_ref, st_ref, cs_ref, hist_ref):
    c = pl.program_id(1)
    row = lax.broadcasted_iota(jnp.int32, (BLOCK, 1), 0)
    valid = jnp.where((c > 0) | (row >= PAD), 1.0, 0.0)

    @pl.when(c == 0)
    def _():
        hist_ref[0:SUBLANES, :] = jnp.zeros((SUBLANES, CONV_DIM), F32)
        st_ref[...] = jnp.zeros_like(st_ref)

    x = xbc_ref[...] * valid
    hist_ref[SUBLANES:SUBLANES + BLOCK, :] = x
    conv = cb_ref[...] + x * cw_ref[SSM_CONV - 1:SSM_CONV, :]
    for j in range(SSM_CONV - 1):
        back = SSM_CONV - 1 - j
        conv = conv + hist_ref[SUBLANES - back:SUBLANES - back + BLOCK, :] * cw_ref[j:j + 1, :]
    tail = x[BLOCK - SUBLANES:, :]
    hist_ref[0:SUBLANES, :] = tail
    cs_ref[...] = tail
    xc = _silu(conv)
    xs = xc[:, :SSM_INNER]
    bm = xc[:, SSM_INNER:SSM_INNER + SSM_GROUPS * SSM_STATE].astype(BF16)
    cm = xc[:, SSM_INNER + SSM_GROUPS * SSM_STATE:].astype(BF16)

    dt = _softplus(dt_ref[...] + dtb_ref[...]) * valid
    la = dt * aneg_ref[...]
    ri = lax.broadcasted_iota(jnp.int32, (BLOCK, BLOCK), 0)
    ci = lax.broadcasted_iota(jnp.int32, (BLOCK, BLOCK), 1)
    causal = ci <= ri
    tri = jnp.where(causal, 1.0, 0.0)
    cum = _hdot(tri, la)
    cum_t = cum.T
    e = e_ref[...]
    xdt = xs * _hdot(dt, e)
    ecum = _hdot(jnp.exp(cum), e)
    cum_last = cum[BLOCK - 1:BLOCK, :]
    xdtw = (xdt * _hdot(jnp.exp(cum_last - cum), e)).astype(BF16)
    xdt_b = xdt.astype(BF16)
    ecl = jnp.exp(cum_t[:, BLOCK - 1:BLOCK])
    fac = _hdot(et_ref[...], jnp.broadcast_to(ecl, (DT_PAD, SSM_STATE)))
    lane = ci

    for g in range(SSM_GROUPS):
        cmg = cm[:, g * SSM_STATE:(g + 1) * SSM_STATE]
        bmg = bm[:, g * SSM_STATE:(g + 1) * SSM_STATE]
        cb = _dot_nt(cmg, bmg)
        for pp in range(2):
            pair = 2 * g + pp
            cols = slice(pair * LANES, (pair + 1) * LANES)
            ys = []
            for hh in range(2):
                h = 2 * pair + hh
                seg = cum[:, h:h + 1] - cum_t[h:h + 1, :]
                m = (cb * jnp.where(causal, jnp.exp(seg), 0.0)).astype(BF16)
                ys.append(_bdot(m, xdt_b[:, cols]))
            y = jnp.where(lane < SSM_HEAD_DIM, ys[0], ys[1])
            st = st_ref[cols, :]
            y = y + _dot_nt(cmg, st.astype(BF16)) * ecum[:, cols]
            st_ref[cols, :] = st * fac[cols, :] + _dot_tn(xdtw[:, cols], bmg)
            y_ref[:, cols] = y + xs[:, cols] * dskip_ref[:, cols]


def _ssd_prompt(xbc, dt, b, cw, cb, dtb, aneg, dskip, e, et):
    m = xbc.shape[0]
    nc = m // b // BLOCK
    row = lambda width: pl.BlockSpec((BLOCK, width), lambda bi, c: (bi * nc + c, 0))
    out_shape = (
        jax.ShapeDtypeStruct((m, SSM_INNER), F32),
        jax.ShapeDtypeStruct((b, SSM_INNER, SSM_STATE), F32),
        jax.ShapeDtypeStruct((b, SUBLANES, CONV_DIM), F32),
    )
    return pl.pallas_call(
        _ssd_kernel,
        grid=(b, nc),
        in_specs=[row(CONV_DIM), row(DT_PAD), _const_spec(cw.shape), _const_spec(cb.shape),
                  _const_spec(dtb.shape), _const_spec(aneg.shape), _const_spec(dskip.shape),
                  _const_spec(e.shape), _const_spec(et.shape)],
        out_specs=(row(SSM_INNER),
                   pl.BlockSpec((None, SSM_INNER, SSM_STATE), lambda bi, c: (bi, 0, 0)),
                   pl.BlockSpec((None, SUBLANES, CONV_DIM), lambda bi, c: (bi, 0, 0))),
        out_shape=out_shape,
        scratch_shapes=[pltpu.VMEM((SUBLANES + BLOCK, CONV_DIM), F32)],
        compiler_params=_cparams(("parallel", "arbitrary")),
        name="ssd_prompt",
    )(xbc, dt, cw, cb, dtb, aneg, dskip, e, et)


def _mix_kernel(xn_ref, h_ref, osb_ref, y_ref, wz_ref, wg_ref, wsb_ref, wssm_ref, wo_ref, snw_ref, n2w_ref,
                h1_ref, xn2_ref):
    xn = xn_ref[...]
    d = h_ref.shape[1]
    z = _bdot(xn, wz_ref[...])
    yz = y_ref[...] * _silu(z)
    gw = SSM_INNER // SSM_GROUPS
    parts = []
    for g in range(SSM_GROUPS):
        yg = yz[:, g * gw:(g + 1) * gw]
        parts.append(yg * lax.rsqrt(jnp.mean(yg * yg, axis=-1, keepdims=True) + EPS))
    yn = (jnp.concatenate(parts, axis=1) * snw_ref[...]).astype(BF16)
    g_a = _sigmoid(_bdot(xn, wg_ref[:, 0:d]))
    merged = g_a * _bdot(osb_ref[...], wsb_ref[...])
    g_s = _sigmoid(_bdot(xn, wg_ref[:, d:2 * d]))
    merged = merged + g_s * _bdot(yn, wssm_ref[...])
    h1 = h_ref[...] + _bdot(merged.astype(BF16), wo_ref[...])
    h1_ref[...] = h1
    xn2_ref[...] = _rms(h1, n2w_ref[...]).astype(BF16)


def _mix_out(xn, h, osb, y, wz, wg, wsb, wssm, wo, snw, n2w, tm):
    m, d = h.shape
    row = lambda width: pl.BlockSpec((tm, width), lambda i: (i, 0))
    return pl.pallas_call(
        _mix_kernel,
        grid=(m // tm,),
        in_specs=[row(d), row(d), row(SB_WIDTH), row(SSM_INNER)] +
                 [_const_spec(a.shape) for a in (wz, wg, wsb, wssm, wo, snw, n2w)],
        out_specs=(row(d), row(d)),
        out_shape=(jax.ShapeDtypeStruct((m, d), F32), jax.ShapeDtypeStruct((m, d), BF16)),
        compiler_params=_cparams(("parallel",)),
        name="mix_out",
    )(xn, h, osb, y, wz, wg, wsb, wssm, wo, snw, n2w)


def _ffn_tail(hdn, h1, wdn_ref, fw_ref, out_ref, final):
    h2 = h1 + _bdot(hdn.astype(BF16), wdn_ref[...])
    out_ref[...] = _rms(h2, fw_ref[...]) if final else h2


def _ffn_prompt_kernel(xn2_ref, h1_ref, wup_ref, wdn_ref, cw_ref, cb_ref, fw_ref, out_ref, fs_ref, hist_ref,
                       *, final):
    t = pl.program_id(1)
    tm = xn2_ref.shape[0]
    f = cw_ref.shape[1]

    @pl.when(t == 0)
    def _():
        hist_ref[0:SUBLANES, :] = jnp.zeros((SUBLANES, f), F32)

    row = lax.broadcasted_iota(jnp.int32, (tm, 1), 0)
    valid = jnp.where((t > 0) | (row >= PAD), 1.0, 0.0)
    xn2 = xn2_ref[...]
    gate = _bdot(xn2, wup_ref[:, 0:f]) * valid
    hist_ref[SUBLANES:SUBLANES + tm, :] = gate
    conv = cb_ref[...] + gate * cw_ref[FFN_CONV - 1:FFN_CONV, :]
    for j in range(FFN_CONV - 1):
        back = FFN_CONV - 1 - j
        conv = conv + hist_ref[SUBLANES - back:SUBLANES - back + tm, :] * cw_ref[j:j + 1, :]
    tail = gate[tm - SUBLANES:, :]
    hist_ref[0:SUBLANES, :] = tail
    fs_ref[...] = tail
    hdn = _silu(conv) * _bdot(xn2, wup_ref[:, f:2 * f])
    _ffn_tail(hdn, h1_ref[...], wdn_ref, fw_ref, out_ref, final)


def _ffn_prompt(xn2, h1, b, wup, wdn, cw, cb, fw, tm, final):
    m, d = h1.shape
    f = cw.shape[1]
    nt = m // b // tm
    row = lambda width: pl.BlockSpec((tm, width), lambda bi, t: (bi * nt + t, 0))
    return pl.pallas_call(
        functools.partial(_ffn_prompt_kernel, final=final),
        grid=(b, nt),
        in_specs=[row(d), row(d)] + [_const_spec(a.shape) for a in (wup, wdn, cw, cb, fw)],
        out_specs=(row(d), pl.BlockSpec((None, SUBLANES, f), lambda bi, t: (bi, 0, 0))),
        out_shape=(jax.ShapeDtypeStruct((m, d), F32), jax.ShapeDtypeStruct((b, SUBLANES, f), F32)),
        scratch_shapes=[pltpu.VMEM((SUBLANES + tm, f), F32)],
        compiler_params=_cparams(("parallel", "arbitrary")),
        name="ffn_prompt",
    )(xn2, h1, wup, wdn, cw, cb, fw)


def _ffn_sample_kernel(xn2_ref, h1_ref, hist_ref, wup_ref, wdn_ref, cw_ref, cb_ref, fw_ref, out_ref, fs_ref,
                       *, final):
    f = cw_ref.shape[1]
    xn2 = xn2_ref[...]
    gate = _bdot(xn2, wup_ref[:, 0:f])
    conv = cb_ref[...] + gate * cw_ref[FFN_CONV - 1:FFN_CONV, :]
    for j in range(FFN_CONV - 1):
        conv = conv + hist_ref[:, j * f:(j + 1) * f] * cw_ref[j:j + 1, :]
    for j in range(FFN_CONV - 2):
        fs_ref[:, j * f:(j + 1) * f] = hist_ref[:, (j + 1) * f:(j + 2) * f]
    fs_ref[:, (FFN_CONV - 2) * f:(FFN_CONV - 1) * f] = gate
    hdn = _silu(conv) * _bdot(xn2, wup_ref[:, f:2 * f])
    _ffn_tail(hdn, h1_ref[...], wdn_ref, fw_ref, out_ref, final)


def _ffn_sample(xn2, h1, hist, wup, wdn, cw, cb, fw, final):
    m, d = h1.shape
    return pl.pallas_call(
        functools.partial(_ffn_sample_kernel, final=final),
        grid=(1,),
        in_specs=[_const_spec(a.shape) for a in (xn2, h1, hist, wup, wdn, cw, cb, fw)],
        out_specs=(_full_spec((m, d)), _full_spec(hist.shape)),
        out_shape=(jax.ShapeDtypeStruct((m, d), F32), jax.ShapeDtypeStruct(hist.shape, F32)),
        compiler_params=_cparams(("arbitrary",)),
        name="ffn_sample",
    )(xn2, h1, hist, wup, wdn, cw, cb, fw)


def _attn_sample_kernel(pt_ref, q_ref, bias_ref, *refs, n_pages):
    k_refs = refs[:n_pages]
    v_refs = refs[n_pages:2 * n_pages]
    o_ref = refs[2 * n_pages]
    hrow = lax.broadcasted_iota(jnp.int32, (SB_HEADS, SB_WIDTH), 0)
    hcol = lax.broadcasted_iota(jnp.int32, (SB_HEADS, SB_WIDTH), 1) // SB_HEAD_DIM
    head_mask = hrow == hcol
    qm = jnp.where(head_mask, jnp.broadcast_to(q_ref[...], (SB_HEADS, SB_WIDTH)), 0.0).astype(BF16)
    bias = bias_ref[...]
    z = jnp.concatenate([_bdot(qm, k_refs[c][...].astype(BF16)) + bias for c in range(n_pages)], axis=0)
    lk = -_softplus(z)
    nr = n_pages * SB_HEADS
    hi, lo = _split_bf16(lk)
    r = _bdot(jnp.concatenate([hi, lo], axis=1), _cumsum_matrix())
    ra = lax.broadcasted_iota(jnp.int32, (nr, nr), 0)
    rb = lax.broadcasted_iota(jnp.int32, (nr, nr), 1)
    mx = jnp.where((ra % SB_HEADS == rb % SB_HEADS) & (rb // SB_HEADS > ra // SB_HEADS), 1.0, 0.0).astype(BF16)
    thi, tlo = _split_bf16(r[:, BLOCK:])
    lt = _bdot(mx, jnp.concatenate([thi, tlo], axis=1))
    w = jnp.exp(z + lk + r[:, :BLOCK] + lt[:, :BLOCK] + lt[:, BLOCK:])
    o_full = jnp.zeros((SB_HEADS, SB_WIDTH), F32)
    for c in range(n_pages):
        wc = w[c * SB_HEADS:(c + 1) * SB_HEADS, :].astype(BF16)
        o_full = o_full + _dot_nt(wc, v_refs[c][...].astype(BF16))
    o_ref[...] = jnp.sum(jnp.where(head_mask, o_full, 0.0), axis=0, keepdims=True)


def _sb_attn_sample(q, bias, cache_kt, cache_vt, page_table, layer):
    s = q.shape[0]
    n_pages = page_table.shape[1]
    page = cache_kt.shape[3]
    assert page == BLOCK and n_pages * SB_HEADS == BLOCK
    q3 = q.reshape(s, 1, SB_WIDTH)
    bias_b = jnp.broadcast_to(bias[:, None], (SB_HEADS, BLOCK))

    def page_spec(c):
        return pl.BlockSpec((None, None, SB_WIDTH, page), lambda i, pt, c=c: (layer, pt[i * n_pages + c], 0, 0))

    grid_spec = pltpu.PrefetchScalarGridSpec(
        num_scalar_prefetch=1,
        grid=(s,),
        in_specs=[pl.BlockSpec((None, 1, SB_WIDTH), lambda i, pt: (i, 0, 0)),
                  pl.BlockSpec((SB_HEADS, BLOCK), lambda i, pt: (0, 0))] +
                 [page_spec(c) for c in range(n_pages)] * 2,
        out_specs=pl.BlockSpec((None, 1, SB_WIDTH), lambda i, pt: (i, 0, 0)),
    )
    out = pl.pallas_call(
        functools.partial(_attn_sample_kernel, n_pages=n_pages),
        grid_spec=grid_spec,
        out_shape=jax.ShapeDtypeStruct((s, 1, SB_WIDTH), F32),
        compiler_params=_cparams(("arbitrary",)),
        name="sb_attn_sample",
    )(page_table.reshape(-1), q3, bias_b, *([cache_kt] * n_pages), *([cache_vt] * n_pages))
    return out.reshape(s, SB_WIDTH)


def _ssd_sample_kernel(xbc_ref, dt_ref, cs_ref, st_ref, cw_ref, cb_ref, dtb_ref, aneg_ref, dskip_ref, e_ref,
                       y_ref, nst_ref, ncs_ref):
    sb = xbc_ref.shape[0]
    x = xbc_ref[...]
    conv = cb_ref[...] + x * cw_ref[SSM_CONV - 1:SSM_CONV, :]
    for j in range(SSM_CONV - 1):
        conv = conv + cs_ref[:, j * CONV_DIM:(j + 1) * CONV_DIM] * cw_ref[j:j + 1, :]
    for j in range(SSM_CONV - 2):
        ncs_ref[:, j * CONV_DIM:(j + 1) * CONV_DIM] = cs_ref[:, (j + 1) * CONV_DIM:(j + 2) * CONV_DIM]
    ncs_ref[:, (SSM_CONV - 2) * CONV_DIM:(SSM_CONV - 1) * CONV_DIM] = x
    xc = _silu(conv)
    xs = xc[:, :SSM_INNER]
    bm = xc[:, SSM_INNER:SSM_INNER + SSM_GROUPS * SSM_STATE]
    cm = xc[:, SSM_INNER + SSM_GROUPS * SSM_STATE:]
    dt = _softplus(dt_ref[...] + dtb_ref[...])
    e = e_ref[...]
    xdt = xs * _hdot(dt, e)
    dec = _hdot(jnp.exp(dt * aneg_ref[...]), e)
    gw = SSM_INNER // SSM_GROUPS
    cb_parts = []
    for g in range(SSM_GROUPS):
        sl = slice(g * SSM_STATE, (g + 1) * SSM_STATE)
        cbg = jnp.sum(cm[:, sl] * bm[:, sl], axis=-1, keepdims=True)
        cb_parts.append(jnp.broadcast_to(cbg, (sb, gw)))
    y = jnp.concatenate(cb_parts, axis=1) * xdt + xs * dskip_ref[...]

    grow = lax.broadcasted_iota(jnp.int32, (SUBLANES, SSM_INNER), 0)
    gcol = lax.broadcasted_iota(jnp.int32, (SUBLANES, SSM_INNER), 1) // gw
    gmask = grow == gcol
    r8 = lax.broadcasted_iota(jnp.int32, (SUBLANES, SSM_STATE), 0)
    ones0 = jnp.where(r8 == 0, 1.0, 0.0)
    row8 = lax.broadcasted_iota(jnp.int32, (SUBLANES, SSM_INNER), 0)
    rsb = lax.broadcasted_iota(jnp.int32, (sb, SSM_INNER), 0)
    ystate = jnp.zeros((sb, SSM_INNER), F32)

    def group_rows(v, i):
        out = jnp.zeros((SUBLANES, SSM_STATE), F32)
        for g in range(SSM_GROUPS):
            vg = jnp.broadcast_to(v[i:i + 1, g * SSM_STATE:(g + 1) * SSM_STATE], (SUBLANES, SSM_STATE))
            out = jnp.where(r8 == g, vg, out)
        return out

    for i in range(sb):
        st = st_ref[i]
        y8 = _dot_nt(group_rows(cm, i).astype(BF16), st.astype(BF16))
        ys = jnp.sum(jnp.where(gmask, y8, 0.0), axis=0, keepdims=True)
        ystate = jnp.where(rsb == i, jnp.broadcast_to(ys, (sb, SSM_INNER)), ystate)
        xg = jnp.where(gmask, jnp.broadcast_to(xdt[i:i + 1, :], (SUBLANES, SSM_INNER)), 0.0)
        d8 = jnp.where(row8 == 0, jnp.broadcast_to(dec[i:i + 1, :], (SUBLANES, SSM_INNER)), 0.0)
        fac = _dot_tn(d8, ones0, precision=HIGHEST)
        inc = _dot_tn(xg, group_rows(bm, i), precision=HIGHEST)
        nst_ref[i] = st * fac + inc
    y_ref[...] = y + ystate * dec


def _ssd_sample(xbc, dt, cs, state, layer, cw, cb, dtb, aneg, dskip, e, sb):
    s = xbc.shape[0]
    row = lambda width: pl.BlockSpec((sb, width), lambda i: (i, 0))
    cs_w = cs.shape[1]
    return pl.pallas_call(
        _ssd_sample_kernel,
        grid=(s // sb,),
        in_specs=[row(CONV_DIM), row(DT_PAD), row(cs_w),
                  pl.BlockSpec((None, sb, SSM_INNER, SSM_STATE), lambda i: (layer, i, 0, 0))] +
                 [_const_spec(a.shape) for a in (cw, cb, dtb, aneg, dskip, e)],
        out_specs=(row(SSM_INNER),
                   pl.BlockSpec((sb, SSM_INNER, SSM_STATE), lambda i: (i, 0, 0)),
                   row(cs_w)),
        out_shape=(jax.ShapeDtypeStruct((s, SSM_INNER), F32),
                   jax.ShapeDtypeStruct((s, SSM_INNER, SSM_STATE), F32),
                   jax.ShapeDtypeStruct((s, cs_w), F32)),
        compiler_params=_cparams(("parallel",)),
        name="ssd_sample",
    )(xbc, dt, cs, state, cw, cb, dtb, aneg, dskip, e)


def _pad_lanes(v, width):
    return jnp.pad(v.astype(F32), (0, width - v.shape[0])).reshape(1, width)


def kernel(x_prompt, x_sample, cache_k, cache_v, state_ssm, state_conv, state_ffn_conv, page_table,
           meta_tokens, norm1_w, w_in, sb_logit_bias, conv_w, conv_b, dt_bias, a_log, d_skip, ssm_norm_w,
           w_sb_out, w_ssm_out, w_out, norm2_w, w_ffn_up, ffn_conv_w, ffn_conv_b, w_ffn_down, final_norm_w):
    b, s, d = x_prompt.shape
    depth = w_in.shape[0]
    n = s + BLOCK
    ns = x_sample.shape[0]
    f = ffn_conv_w.shape[2]
    assert x_sample.shape[1] == 1 and n % BLOCK == 0
    tm_p = 512
    tm_f = n // 8
    assert (b * n) % tm_p == 0 and tm_f % SUBLANES == 0

    hp = jnp.concatenate([jnp.zeros((b, PAD, d), F32),
                          jnp.broadcast_to(meta_tokens.astype(F32)[None], (b, N_META, d)),
                          x_prompt], axis=1).reshape(b * n, d)
    hs = x_sample.reshape(ns, d)

    hid = jnp.arange(SSM_INNER, dtype=jnp.int32) // SSM_HEAD_DIM
    e_mat = (jnp.arange(DT_PAD, dtype=jnp.int32)[:, None] == hid[None, :]).astype(F32)
    et_mat = e_mat.T
    state4 = state_ssm.reshape(depth, ns, SSM_INNER, SSM_STATE)
    pool, page = cache_k.shape[1], cache_k.shape[2]
    cache_kt = jnp.transpose(cache_k, (0, 1, 3, 4, 2)).reshape(depth, pool, SB_WIDTH, page)
    cache_vt = jnp.transpose(cache_v, (0, 1, 3, 4, 2)).reshape(depth, pool, SB_WIDTH, page)

    o_qkv = 3 * SB_WIDTH
    o_z = o_qkv
    o_xbc = o_z + SSM_INNER
    o_dt = o_xbc + CONV_DIM
    o_g = o_dt + SSM_HEADS

    outs = {k: [] for k in ("kp", "vp", "sp", "cp", "fp", "ks", "vs", "ss", "cs", "fs")}
    for l in range(depth):
        final = l == depth - 1
        wl = w_in[l]
        w_a = jnp.concatenate([wl[:, :o_qkv], wl[:, o_xbc:o_dt],
                               jnp.pad(wl[:, o_dt:o_g], ((0, 0), (0, DT_PAD - SSM_HEADS)))], axis=1).astype(BF16)
        wz = wl[:, o_z:o_xbc].astype(BF16)
        wg = wl[:, o_g:].astype(BF16)
        wsb = w_sb_out[l].astype(BF16)
        wssm = w_ssm_out[l].astype(BF16)
        wo = w_out[l].astype(BF16)
        wup = w_ffn_up[l].astype(BF16)
        wdn = w_ffn_down[l].astype(BF16)
        n1w = norm1_w[l].reshape(1, d)
        n2w = norm2_w[l].reshape(1, d)
        fw = final_norm_w.reshape(1, d)
        snw = ssm_norm_w[l].reshape(1, SSM_INNER)
        cw = conv_w[l]
        cb = conv_b[l].reshape(1, CONV_DIM)
        dtb = _pad_lanes(dt_bias[l], DT_PAD)
        aneg = _pad_lanes(-jnp.exp(a_log[l].astype(F32)), DT_PAD)
        dskip = jnp.repeat(d_skip[l].astype(F32), SSM_HEAD_DIM).reshape(1, SSM_INNER)
        fcw = ffn_conv_w[l]
        fcb = ffn_conv_b[l].reshape(1, f)
        bias = sb_logit_bias[l].astype(F32)

        q, kb, vb, k32, v32, xbc, dt, xn = _inproj(hp, n1w, w_a, tm_p)
        osb = _sb_attn_prompt(q.reshape(b, n, SB_WIDTH), kb.reshape(b, n, SB_WIDTH),
                              vb.reshape(b, n, SB_WIDTH), bias).reshape(b * n, SB_WIDTH)
        y, st, cst = _ssd_prompt(xbc, dt, b, cw, cb, dtb, aneg, dskip, e_mat, et_mat)
        h1, xn2 = _mix_out(xn, hp, osb, y, wz, wg, wsb, wssm, wo, snw, n2w, tm_p)
        hp, fst = _ffn_prompt(xn2, h1, b, wup, wdn, fcw, fcb, fw, tm_f, final)
        outs["kp"].append(k32.reshape(b, n, SB_HEADS, SB_HEAD_DIM)[:, PAD:])
        outs["vp"].append(v32.reshape(b, n, SB_HEADS, SB_HEAD_DIM)[:, PAD:])
        outs["sp"].append(st.reshape(b, SSM_HEADS, SSM_HEAD_DIM, SSM_STATE))
        outs["cp"].append(cst[:, SUBLANES - (SSM_CONV - 1):])
        outs["fp"].append(fst[:, SUBLANES - (FFN_CONV - 1):])

        q, kb, vb, k32, v32, xbc, dt, xn = _inproj(hs, n1w, w_a, ns)
        osb = _sb_attn_sample(q.astype(F32), bias, cache_kt, cache_vt, page_table, l).astype(BF16)
        cs_in = state_conv[l].reshape(ns, (SSM_CONV - 1) * CONV_DIM)
        y, nst, ncs = _ssd_sample(xbc, dt, cs_in, state4, l, cw, cb, dtb, aneg, dskip, e_mat, SUBLANES)
        h1, xn2 = _mix_out(xn, hs, osb, y, wz, wg, wsb, wssm, wo, snw, n2w, ns)
        fs_in = state_ffn_conv[l].reshape(ns, (FFN_CONV - 1) * f)
        hs, nfs = _ffn_sample(xn2, h1, fs_in, wup, wdn, fcw, fcb, fw, final)
        outs["ks"].append(k32.reshape(ns, 1, SB_HEADS, SB_HEAD_DIM))
        outs["vs"].append(v32.reshape(ns, 1, SB_HEADS, SB_HEAD_DIM))
        outs["ss"].append(nst.reshape(ns, SSM_HEADS, SSM_HEAD_DIM, SSM_STATE))
        outs["cs"].append(ncs.reshape(ns, SSM_CONV - 1, CONV_DIM))
        outs["fs"].append(nfs.reshape(ns, FFN_CONV - 1, f))

    y_prompt = hp.reshape(b, n, d)[:, BLOCK:]
    y_sample = hs.reshape(ns, 1, d)
    st = {k: jnp.stack(v) for k, v in outs.items()}
    return (y_prompt, y_sample, st["kp"], st["vp"], st["ks"], st["vs"],
            st["sp"], st["ss"], st["cp"], st["cs"], st["fp"], st["fs"])
```

```python
import functools

import jax
import jax.numpy as jnp
from jax import lax
from jax.experimental import pallas as pl
from jax.experimental.pallas import tpu as pltpu

F32 = jnp.float32
BF16 = jnp.bfloat16
HIGHEST = lax.Precision.HIGHEST

N_META = 16
BLOCK = 128
PAD = BLOCK - N_META
SB_HEADS = 8
SB_HEAD_DIM = 64
SB_WIDTH = SB_HEADS * SB_HEAD_DIM
SB_SCALE = SB_HEAD_DIM ** -0.5
SSM_HEADS = 16
SSM_HEAD_DIM = 64
SSM_INNER = SSM_HEADS * SSM_HEAD_DIM
SSM_GROUPS = 4
SSM_STATE = 128
SSM_CONV = 4
CONV_DIM = SSM_INNER + 2 * SSM_GROUPS * SSM_STATE
FFN_CONV = 3
EPS = 1e-6
LANES = 128
SUBLANES = 8
DT_PAD = LANES
ATTN_HEADS = 8
ATTN_UNROLL = 4
VMEM_LIMIT = 56 * 1024 * 1024


def _cparams(sem):
    return pltpu.CompilerParams(dimension_semantics=sem, vmem_limit_bytes=VMEM_LIMIT)


def _const_spec(shape):
    nd = len(shape)
    return pl.BlockSpec(shape, lambda *_: (0,) * nd, pipeline_mode=pl.Buffered(1))


def _full_spec(shape):
    nd = len(shape)
    return pl.BlockSpec(shape, lambda *_: (0,) * nd)


def _bdot(a, b):
    return jnp.dot(a, b, preferred_element_type=F32)


def _dot_nt(a, b):
    return lax.dot_general(a, b, (((1,), (1,)), ((), ())), preferred_element_type=F32)


def _dot_tn(a, b, precision=None):
    return lax.dot_general(a, b, (((0,), (0,)), ((), ())), preferred_element_type=F32, precision=precision)


def _hdot(a, b):
    return jnp.dot(a, b, preferred_element_type=F32, precision=HIGHEST)


def _sigmoid(x):
    return 1.0 / (1.0 + jnp.exp(-x))


def _silu(x):
    return x * _sigmoid(x)


def _softplus(x):
    return jnp.maximum(x, 0.0) + jnp.log(1.0 + jnp.exp(-jnp.abs(x)))


def _rms(x, w):
    return x * lax.rsqrt(jnp.mean(x * x, axis=-1, keepdims=True) + EPS) * w


def _split_bf16(x):
    hi = x.astype(BF16)
    lo = (x - hi.astype(F32)).astype(BF16)
    return hi, lo


def _inproj_kernel(h_ref, nw_ref, w_ref, q_ref, kb_ref, vb_ref, k_ref, v_ref, xbc_ref, dt_ref, xn_ref):
    xn = _rms(h_ref[...], nw_ref[...]).astype(BF16)
    xn_ref[...] = xn
    qkv = _bdot(xn, w_ref[:, 0:3 * SB_WIDTH])
    q_ref[...] = (qkv[:, 0:SB_WIDTH] * SB_SCALE).astype(BF16)
    k = qkv[:, SB_WIDTH:2 * SB_WIDTH]
    v = qkv[:, 2 * SB_WIDTH:3 * SB_WIDTH]
    k_ref[...] = k
    v_ref[...] = v
    kb_ref[...] = k.astype(BF16)
    vb_ref[...] = v.astype(BF16)
    o = 3 * SB_WIDTH
    xbc_ref[...] = _bdot(xn, w_ref[:, o:o + CONV_DIM])
    dt_ref[...] = _bdot(xn, w_ref[:, o + CONV_DIM:o + CONV_DIM + DT_PAD])


def _inproj(h, norm_w, w_a, tm):
    m, d = h.shape
    wa = w_a.shape[1]
    row = lambda width: pl.BlockSpec((tm, width), lambda i: (i, 0))
    out_shape = (
        jax.ShapeDtypeStruct((m, SB_WIDTH), BF16),
        jax.ShapeDtypeStruct((m, SB_WIDTH), BF16),
        jax.ShapeDtypeStruct((m, SB_WIDTH), BF16),
        jax.ShapeDtypeStruct((m, SB_WIDTH), F32),
        jax.ShapeDtypeStruct((m, SB_WIDTH), F32),
        jax.ShapeDtypeStruct((m, CONV_DIM), F32),
        jax.ShapeDtypeStruct((m, DT_PAD), F32),
        jax.ShapeDtypeStruct((m, d), BF16),
    )
    return pl.pallas_call(
        _inproj_kernel,
        grid=(m // tm,),
        in_specs=[row(d), _const_spec((1, d)), _const_spec((d, wa))],
        out_specs=(row(SB_WIDTH), row(SB_WIDTH), row(SB_WIDTH), row(SB_WIDTH), row(SB_WIDTH),
                   row(CONV_DIM), row(DT_PAD), row(d)),
        out_shape=out_shape,
        compiler_params=_cparams(("parallel",)),
        name="inproj",
    )(h, norm_w, w_a)


def _cumsum_matrix():
    r = lax.broadcasted_iota(jnp.int32, (2 * BLOCK, 2 * BLOCK), 0) % BLOCK
    c = lax.broadcasted_iota(jnp.int32, (2 * BLOCK, 2 * BLOCK), 1)
    return jnp.where((r > c) | (c >= BLOCK), 1.0, 0.0).astype(BF16)


def _attn_kernel(bias_ref, q_ref, k_ref, v_ref, o_ref, acc_ref, carry_ref):
    hg = pl.program_id(1)
    qi = pl.program_id(2)
    lane = lax.broadcasted_iota(jnp.int32, (BLOCK, BLOCK), 1)
    uu = _cumsum_matrix()
    q_heads, biases = [], []
    for h in range(ATTN_HEADS):
        q = q_ref[:, (h // 2) * LANES:(h // 2 + 1) * LANES]
        keep = (lane < SB_HEAD_DIM) if h % 2 == 0 else (lane >= SB_HEAD_DIM)
        q_heads.append(jnp.where(keep, q, jnp.zeros_like(q)))
        biases.append(bias_ref[hg * ATTN_HEADS + h])

    def process(blk, nblk, masked):
        width = nblk * BLOCK
        start = blk * BLOCK if isinstance(blk, int) else pl.multiple_of(blk * BLOCK, BLOCK)
        if masked:
            t_pos = qi * BLOCK + lax.broadcasted_iota(jnp.int32, (BLOCK, width), 0)
            s_pos = start + lax.broadcasted_iota(jnp.int32, (BLOCK, width), 1)
            mask = (s_pos < t_pos) & (s_pos >= PAD)
        for h in range(ATTN_HEADS):
            cols = slice((h // 2) * LANES, (h // 2 + 1) * LANES)
            z = _dot_nt(q_heads[h], k_ref[pl.ds(start, width), cols]) + biases[h]
            lk = -_softplus(z)
            if masked:
                lk = jnp.where(mask, lk, 0.0)
            hi, lo = _split_bf16(lk)
            carry = carry_ref[h]
            between = [None] * nblk
            for j in reversed(range(nblk)):
                sl = slice(j * BLOCK, (j + 1) * BLOCK)
                r = _bdot(jnp.concatenate([hi[:, sl], lo[:, sl]], axis=1), uu)
                between[j] = r[:, :BLOCK] + carry
                carry = carry + r[:, BLOCK:]
            carry_ref[h] = carry
            w = jnp.exp(z + lk + jnp.concatenate(between, axis=1))
            if masked:
                w = jnp.where(mask, w, 0.0)
            acc_ref[h] += _bdot(w.astype(BF16), v_ref[pl.ds(start, width), cols])

    acc_ref[...] = jnp.zeros_like(acc_ref)
    carry_ref[...] = jnp.zeros_like(carry_ref)
    process(qi, 1, True)
    n_inner = jnp.maximum(qi - 1, 0)
    n_full = n_inner // ATTN_UNROLL

    def body(i, _):
        process(qi - ATTN_UNROLL * (i + 1), ATTN_UNROLL, False)
        return 0

    lax.fori_loop(0, n_full, body, 0)
    rem = n_inner - ATTN_UNROLL * n_full
    for r in range(ATTN_UNROLL):
        pl.when((qi > 0) & (rem == r))(functools.partial(process, 0, r + 1, True))

    for p in range(ATTN_HEADS // 2):
        o_ref[:, p * LANES:(p + 1) * LANES] = jnp.where(
            lane < SB_HEAD_DIM, acc_ref[2 * p], acc_ref[2 * p + 1]).astype(o_ref.dtype)


def _sb_attn_prompt(q, k, v, bias):
    b, n, _ = q.shape
    nb = n // BLOCK
    gw = ATTN_HEADS * SB_HEAD_DIM
    return pl.pallas_call(
        _attn_kernel,
        grid=(b, SB_WIDTH // gw, nb),
        in_specs=[
            pl.BlockSpec(memory_space=pltpu.SMEM),
            pl.BlockSpec((None, BLOCK, gw), lambda bi, p, qi: (bi, qi, p)),
            pl.BlockSpec((None, n, gw), lambda bi, p, qi: (bi, 0, p)),
            pl.BlockSpec((None, n, gw), lambda bi, p, qi: (bi, 0, p)),
        ],
        out_specs=pl.BlockSpec((None, BLOCK, gw), lambda bi, p, qi: (bi, qi, p)),
        scratch_shapes=[pltpu.VMEM((ATTN_HEADS, BLOCK, BLOCK), F32), pltpu.VMEM((ATTN_HEADS, BLOCK, BLOCK), F32)],
        out_shape=jax.ShapeDtypeStruct((b, n, SB_WIDTH), BF16),
        compiler_params=_cparams(("parallel", "parallel", "arbitrary")),
        name="sb_attn_prompt",
    )(bias, q, k, v)


def _shift_rows(x, prev_tail, k):
    rolled = pltpu.roll(x, k, axis=0)
    fix = pltpu.roll(prev_tail, k, axis=0)
    row = lax.broadcasted_iota(jnp.int32, prev_tail.shape, 0)
    head = jnp.where(row < k, fix, rolled[:SUBLANES])
    return jnp.concatenate([head, rolled[SUBLANES:]], axis=0)


def _expand_heads(x, e3):
    lane = lax.broadcasted_iota(jnp.int32, x.shape, 1)
    x = jnp.where(lane < SSM_HEADS, x, 0.0)
    hi = x.astype(BF16).astype(F32)
    r1 = x - hi
    mid = r1.astype(BF16).astype(F32)
    lo = r1 - mid
    packed = hi + pltpu.roll(mid, SSM_HEADS, axis=1) + pltpu.roll(lo, 2 * SSM_HEADS, axis=1)
    return _bdot(packed.astype(BF16), e3)


def _ssd_kernel(xbc_ref, dt_ref, cw_ref, cb_ref, dtb_ref, aneg_ref, dskip_ref, e3_ref,
                y_ref, st_ref, cs_ref, hist_ref):
    c = pl.program_id(1)
    row = lax.broadcasted_iota(jnp.int32, (BLOCK, 1), 0)
    valid = jnp.where((c > 0) | (row >= PAD), 1.0, 0.0)

    @pl.when(c == 0)
    def _():
        hist_ref[...] = jnp.zeros_like(hist_ref)
        st_ref[...] = jnp.zeros_like(st_ref)

    x = xbc_ref[...] * valid
    prev_tail = hist_ref[...]
    conv = cb_ref[...] + x * cw_ref[SSM_CONV - 1:SSM_CONV, :]
    for j in range(SSM_CONV - 1):
        conv = conv + _shift_rows(x, prev_tail, SSM_CONV - 1 - j) * cw_ref[j:j + 1, :]
    tail = x[BLOCK - SUBLANES:, :]
    hist_ref[...] = tail
    cs_ref[...] = tail
    xc = _silu(conv)
    xs = xc[:, :SSM_INNER]
    bm = xc[:, SSM_INNER:SSM_INNER + SSM_GROUPS * SSM_STATE].astype(BF16)
    cm = xc[:, SSM_INNER + SSM_GROUPS * SSM_STATE:].astype(BF16)

    dt = _softplus(dt_ref[...] + dtb_ref[...]) * valid
    la = dt * aneg_ref[...]
    ri = lax.broadcasted_iota(jnp.int32, (BLOCK, BLOCK), 0)
    ci = lax.broadcasted_iota(jnp.int32, (BLOCK, BLOCK), 1)
    causal = ci <= ri
    tri = jnp.where(causal, 1.0, 0.0)
    cum = _hdot(tri, la)
    cum_t = cum.T
    e3 = e3_ref[...]
    xdt = xs * _expand_heads(dt, e3)
    ecum = _expand_heads(jnp.exp(cum), e3)
    cum_last = cum[BLOCK - 1:BLOCK, :]
    xdtw = (xdt * _expand_heads(jnp.exp(cum_last - cum), e3)).astype(BF16)
    xdt_b = xdt.astype(BF16)
    ecl = jnp.exp(cum_last)
    lane = ci

    for g in range(SSM_GROUPS):
        cmg = cm[:, g * SSM_STATE:(g + 1) * SSM_STATE]
        bmg = bm[:, g * SSM_STATE:(g + 1) * SSM_STATE]
        cb = _dot_nt(cmg, bmg)
        for pp in range(2):
            pair = 2 * g + pp
            cols = slice(pair * LANES, (pair + 1) * LANES)
            ys = []
            for hh in range(2):
                h = 2 * pair + hh
                seg = cum[:, h:h + 1] - cum_t[h:h + 1, :]
                m = (cb * jnp.where(causal, jnp.exp(seg), 0.0)).astype(BF16)
                ys.append(_bdot(m, xdt_b[:, cols]))
            y = jnp.where(lane < SSM_HEAD_DIM, ys[0], ys[1])
            st = st_ref[cols, :]
            y = y + _dot_nt(cmg, st.astype(BF16)) * ecum[:, cols]
            fac = jnp.where(ri < SSM_HEAD_DIM,
                            jnp.broadcast_to(ecl[:, 2 * pair:2 * pair + 1], (LANES, SSM_STATE)),
                            jnp.broadcast_to(ecl[:, 2 * pair + 1:2 * pair + 2], (LANES, SSM_STATE)))
            st_ref[cols, :] = st * fac + _dot_tn(xdtw[:, cols], bmg)
            y_ref[:, cols] = y + xs[:, cols] * dskip_ref[:, cols]


def _ssd_prompt(xbc, dt, b, cw, cb, dtb, aneg, dskip, e3):
    m = xbc.shape[0]
    nc = m // b // BLOCK
    row = lambda width: pl.BlockSpec((BLOCK, width), lambda bi, c: (bi * nc + c, 0))
    out_shape = (
        jax.ShapeDtypeStruct((m, SSM_INNER), F32),
        jax.ShapeDtypeStruct((b, SSM_INNER, SSM_STATE), F32),
        jax.ShapeDtypeStruct((b, SUBLANES, CONV_DIM), F32),
    )
    return pl.pallas_call(
        _ssd_kernel,
        grid=(b, nc),
        in_specs=[row(CONV_DIM), row(DT_PAD), _const_spec(cw.shape), _const_spec(cb.shape),
                  _const_spec(dtb.shape), _const_spec(aneg.shape), _const_spec(dskip.shape),
                  _const_spec(e3.shape)],
        out_specs=(row(SSM_INNER),
                   pl.BlockSpec((None, SSM_INNER, SSM_STATE), lambda bi, c: (bi, 0, 0)),
                   pl.BlockSpec((None, SUBLANES, CONV_DIM), lambda bi, c: (bi, 0, 0))),
        out_shape=out_shape,
        scratch_shapes=[pltpu.VMEM((SUBLANES, CONV_DIM), F32)],
        compiler_params=_cparams(("parallel", "arbitrary")),
        name="ssd_prompt",
    )(xbc, dt, cw, cb, dtb, aneg, dskip, e3)


def _mix_kernel(xn_ref, h_ref, osb_ref, y_ref, wz_ref, wg_ref, wsb_ref, wssm_ref, wo_ref, snw_ref, n2w_ref,
                h1_ref, xn2_ref):
    xn = xn_ref[...]
    d = h_ref.shape[1]
    z = _bdot(xn, wz_ref[...])
    yz = y_ref[...] * _silu(z)
    gw = SSM_INNER // SSM_GROUPS
    parts = []
    for g in range(SSM_GROUPS):
        yg = yz[:, g * gw:(g + 1) * gw]
        parts.append(yg * lax.rsqrt(jnp.mean(yg * yg, axis=-1, keepdims=True) + EPS))
    yn = (jnp.concatenate(parts, axis=1) * snw_ref[...]).astype(BF16)
    g_a = _sigmoid(_bdot(xn, wg_ref[:, 0:d]))
    merged = g_a * _bdot(osb_ref[...], wsb_ref[...])
    g_s = _sigmoid(_bdot(xn, wg_ref[:, d:2 * d]))
    merged = merged + g_s * _bdot(yn, wssm_ref[...])
    h1 = h_ref[...] + _bdot(merged.astype(BF16), wo_ref[...])
    h1_ref[...] = h1
    xn2_ref[...] = _rms(h1, n2w_ref[...]).astype(BF16)


def _mix_out(xn, h, osb, y, wz, wg, wsb, wssm, wo, snw, n2w, tm):
    m, d = h.shape
    row = lambda width: pl.BlockSpec((tm, width), lambda i: (i, 0))
    return pl.pallas_call(
        _mix_kernel,
        grid=(m // tm,),
        in_specs=[row(d), row(d), row(SB_WIDTH), row(SSM_INNER)] +
                 [_const_spec(a.shape) for a in (wz, wg, wsb, wssm, wo, snw, n2w)],
        out_specs=(row(d), row(d)),
        out_shape=(jax.ShapeDtypeStruct((m, d), F32), jax.ShapeDtypeStruct((m, d), BF16)),
        compiler_params=_cparams(("parallel",)),
        name="mix_out",
    )(xn, h, osb, y, wz, wg, wsb, wssm, wo, snw, n2w)


def _ffn_tail(hdn, h1, wdn_ref, fw_ref, out_ref, final):
    h2 = h1 + _bdot(hdn.astype(BF16), wdn_ref[...])
    out_ref[...] = _rms(h2, fw_ref[...]) if final else h2


def _ffn_prompt_kernel(xn2_ref, h1_ref, wup_ref, wdn_ref, cw_ref, cb_ref, fw_ref, out_ref, fs_ref, hist_ref,
                       *, final):
    t = pl.program_id(1)
    tm = xn2_ref.shape[0]
    f = cw_ref.shape[1]

    @pl.when(t == 0)
    def _():
        hist_ref[0:SUBLANES, :] = jnp.zeros((SUBLANES, f), F32)

    row = lax.broadcasted_iota(jnp.int32, (tm, 1), 0)
    valid = jnp.where((t > 0) | (row >= PAD), 1.0, 0.0)
    xn2 = xn2_ref[...]
    gate = _bdot(xn2, wup_ref[:, 0:f]) * valid
    hist_ref[SUBLANES:SUBLANES + tm, :] = gate
    conv = cb_ref[...] + gate * cw_ref[FFN_CONV - 1:FFN_CONV, :]
    for j in range(FFN_CONV - 1):
        back = FFN_CONV - 1 - j
        conv = conv + hist_ref[SUBLANES - back:SUBLANES - back + tm, :] * cw_ref[j:j + 1, :]
    tail = gate[tm - SUBLANES:, :]
    hist_ref[0:SUBLANES, :] = tail
    fs_ref[...] = tail
    hdn = _silu(conv) * _bdot(xn2, wup_ref[:, f:2 * f])
    _ffn_tail(hdn, h1_ref[...], wdn_ref, fw_ref, out_ref, final)


def _ffn_prompt(xn2, h1, b, wup, wdn, cw, cb, fw, tm, final):
    m, d = h1.shape
    f = cw.shape[1]
    nt = m // b // tm
    row = lambda width: pl.BlockSpec((tm, width), lambda bi, t: (bi * nt + t, 0))
    return pl.pallas_call(
        functools.partial(_ffn_prompt_kernel, final=final),
        grid=(b, nt),
        in_specs=[row(d), row(d)] + [_const_spec(a.shape) for a in (wup, wdn, cw, cb, fw)],
        out_specs=(row(d), pl.BlockSpec((None, SUBLANES, f), lambda bi, t: (bi, 0, 0))),
        out_shape=(jax.ShapeDtypeStruct((m, d), F32), jax.ShapeDtypeStruct((b, SUBLANES, f), F32)),
        scratch_shapes=[pltpu.VMEM((SUBLANES + tm, f), F32)],
        compiler_params=_cparams(("parallel", "arbitrary")),
        name="ffn_prompt",
    )(xn2, h1, wup, wdn, cw, cb, fw)


def _ffn_sample_kernel(xn2_ref, h1_ref, hist_ref, wup_ref, wdn_ref, cw_ref, cb_ref, fw_ref, out_ref, fs_ref,
                       *, final):
    f = cw_ref.shape[1]
    xn2 = xn2_ref[...]
    gate = _bdot(xn2, wup_ref[:, 0:f])
    conv = cb_ref[...] + gate * cw_ref[FFN_CONV - 1:FFN_CONV, :]
    for j in range(FFN_CONV - 1):
        conv = conv + hist_ref[:, j * f:(j + 1) * f] * cw_ref[j:j + 1, :]
    for j in range(FFN_CONV - 2):
        fs_ref[:, j * f:(j + 1) * f] = hist_ref[:, (j + 1) * f:(j + 2) * f]
    fs_ref[:, (FFN_CONV - 2) * f:(FFN_CONV - 1) * f] = gate
    hdn = _silu(conv) * _bdot(xn2, wup_ref[:, f:2 * f])
    _ffn_tail(hdn, h1_ref[...], wdn_ref, fw_ref, out_ref, final)


def _ffn_sample(xn2, h1, hist, wup, wdn, cw, cb, fw, final):
    m, d = h1.shape
    return pl.pallas_call(
        functools.partial(_ffn_sample_kernel, final=final),
        grid=(1,),
        in_specs=[_const_spec(a.shape) for a in (xn2, h1, hist, wup, wdn, cw, cb, fw)],
        out_specs=(_full_spec((m, d)), _full_spec(hist.shape)),
        out_shape=(jax.ShapeDtypeStruct((m, d), F32), jax.ShapeDtypeStruct(hist.shape, F32)),
        compiler_params=_cparams(("arbitrary",)),
        name="ffn_sample",
    )(xn2, h1, hist, wup, wdn, cw, cb, fw)


def _attn_sample_kernel(pt_ref, q_ref, bias_ref, *refs, n_pages):
    k_refs = refs[:n_pages]
    v_refs = refs[n_pages:2 * n_pages]
    o_ref = refs[2 * n_pages]
    hrow = lax.broadcasted_iota(jnp.int32, (SB_HEADS, SB_WIDTH), 0)
    hcol = lax.broadcasted_iota(jnp.int32, (SB_HEADS, SB_WIDTH), 1) // SB_HEAD_DIM
    head_mask = hrow == hcol
    qm = jnp.where(head_mask, jnp.broadcast_to(q_ref[...], (SB_HEADS, SB_WIDTH)), 0.0).astype(BF16)
    bias = bias_ref[...]
    z = jnp.concatenate([_bdot(qm, k_refs[c][...].astype(BF16)) + bias for c in range(n_pages)], axis=0)
    lk = -_softplus(z)
    nr = n_pages * SB_HEADS
    hi, lo = _split_bf16(lk)
    r = _bdot(jnp.concatenate([hi, lo], axis=1), _cumsum_matrix())
    ra = lax.broadcasted_iota(jnp.int32, (nr, nr), 0)
    rb = lax.broadcasted_iota(jnp.int32, (nr, nr), 1)
    mx = jnp.where((ra % SB_HEADS == rb % SB_HEADS) & (rb // SB_HEADS > ra // SB_HEADS), 1.0, 0.0).astype(BF16)
    thi, tlo = _split_bf16(r[:, BLOCK:])
    lt = _bdot(mx, jnp.concatenate([thi, tlo], axis=1))
    w = jnp.exp(z + lk + r[:, :BLOCK] + lt[:, :BLOCK] + lt[:, BLOCK:])
    o_full = jnp.zeros((SB_HEADS, SB_WIDTH), F32)
    for c in range(n_pages):
        wc = w[c * SB_HEADS:(c + 1) * SB_HEADS, :].astype(BF16)
        o_full = o_full + _dot_nt(wc, v_refs[c][...].astype(BF16))
    o_ref[...] = jnp.sum(jnp.where(head_mask, o_full, 0.0), axis=0, keepdims=True)


def _sb_attn_sample(q, bias, cache_kt, cache_vt, page_table, layer):
    s = q.shape[0]
    n_pages = page_table.shape[1]
    page = cache_kt.shape[3]
    assert page == BLOCK and n_pages * SB_HEADS == BLOCK
    q3 = q.reshape(s, 1, SB_WIDTH)
    bias_b = jnp.broadcast_to(bias[:, None], (SB_HEADS, BLOCK))

    def page_spec(c):
        return pl.BlockSpec((None, None, SB_WIDTH, page), lambda i, pt, c=c: (layer, pt[i * n_pages + c], 0, 0))

    grid_spec = pltpu.PrefetchScalarGridSpec(
        num_scalar_prefetch=1,
        grid=(s,),
        in_specs=[pl.BlockSpec((None, 1, SB_WIDTH), lambda i, pt: (i, 0, 0)),
                  pl.BlockSpec((SB_HEADS, BLOCK), lambda i, pt: (0, 0))] +
                 [page_spec(c) for c in range(n_pages)] * 2,
        out_specs=pl.BlockSpec((None, 1, SB_WIDTH), lambda i, pt: (i, 0, 0)),
    )
    out = pl.pallas_call(
        functools.partial(_attn_sample_kernel, n_pages=n_pages),
        grid_spec=grid_spec,
        out_shape=jax.ShapeDtypeStruct((s, 1, SB_WIDTH), F32),
        compiler_params=_cparams(("arbitrary",)),
        name="sb_attn_sample",
    )(page_table.reshape(-1), q3, bias_b, *([cache_kt] * n_pages), *([cache_vt] * n_pages))
    return out.reshape(s, SB_WIDTH)


def _ssd_sample_kernel(xbc_ref, dt_ref, cs_ref, st_ref, cw_ref, cb_ref, dtb_ref, aneg_ref, dskip_ref, e_ref,
                       y_ref, nst_ref, ncs_ref):
    sb = xbc_ref.shape[0]
    x = xbc_ref[...]
    conv = cb_ref[...] + x * cw_ref[SSM_CONV - 1:SSM_CONV, :]
    for j in range(SSM_CONV - 1):
        conv = conv + cs_ref[:, j * CONV_DIM:(j + 1) * CONV_DIM] * cw_ref[j:j + 1, :]
    for j in range(SSM_CONV - 2):
        ncs_ref[:, j * CONV_DIM:(j + 1) * CONV_DIM] = cs_ref[:, (j + 1) * CONV_DIM:(j + 2) * CONV_DIM]
    ncs_ref[:, (SSM_CONV - 2) * CONV_DIM:(SSM_CONV - 1) * CONV_DIM] = x
    xc = _silu(conv)
    xs = xc[:, :SSM_INNER]
    bm = xc[:, SSM_INNER:SSM_INNER + SSM_GROUPS * SSM_STATE]
    cm = xc[:, SSM_INNER + SSM_GROUPS * SSM_STATE:]
    dt = _softplus(dt_ref[...] + dtb_ref[...])
    e = e_ref[...]
    xdt = xs * _hdot(dt, e)
    dec = _hdot(jnp.exp(dt * aneg_ref[...]), e)
    gw = SSM_INNER // SSM_GROUPS
    cb_parts = []
    for g in range(SSM_GROUPS):
        sl = slice(g * SSM_STATE, (g + 1) * SSM_STATE)
        cbg = jnp.sum(cm[:, sl] * bm[:, sl], axis=-1, keepdims=True)
        cb_parts.append(jnp.broadcast_to(cbg, (sb, gw)))
    y = jnp.concatenate(cb_parts, axis=1) * xdt + xs * dskip_ref[...]

    grow = lax.broadcasted_iota(jnp.int32, (SUBLANES, SSM_INNER), 0)
    gcol = lax.broadcasted_iota(jnp.int32, (SUBLANES, SSM_INNER), 1) // gw
    gmask = grow == gcol
    r8 = lax.broadcasted_iota(jnp.int32, (SUBLANES, SSM_STATE), 0)
    ones0 = jnp.where(r8 == 0, 1.0, 0.0)
    row8 = lax.broadcasted_iota(jnp.int32, (SUBLANES, SSM_INNER), 0)
    rsb = lax.broadcasted_iota(jnp.int32, (sb, SSM_INNER), 0)
    ystate = jnp.zeros((sb, SSM_INNER), F32)

    def group_rows(v, i):
        out = jnp.zeros((SUBLANES, SSM_STATE), F32)
        for g in range(SSM_GROUPS):
            vg = jnp.broadcast_to(v[i:i + 1, g * SSM_STATE:(g + 1) * SSM_STATE], (SUBLANES, SSM_STATE))
            out = jnp.where(r8 == g, vg, out)
        return out

    for i in range(sb):
        st = st_ref[i]
        y8 = _dot_nt(group_rows(cm, i).astype(BF16), st.astype(BF16))
        ys = jnp.sum(jnp.where(gmask, y8, 0.0), axis=0, keepdims=True)
        ystate = jnp.where(rsb == i, jnp.broadcast_to(ys, (sb, SSM_INNER)), ystate)
        xg = jnp.where(gmask, jnp.broadcast_to(xdt[i:i + 1, :], (SUBLANES, SSM_INNER)), 0.0)
        d8 = jnp.where(row8 == 0, jnp.broadcast_to(dec[i:i + 1, :], (SUBLANES, SSM_INNER)), 0.0)
        fac = _dot_tn(d8, ones0, precision=HIGHEST)
        inc = _dot_tn(xg, group_rows(bm, i), precision=HIGHEST)
        nst_ref[i] = st * fac + inc
    y_ref[...] = y + ystate * dec


def _ssd_sample(xbc, dt, cs, state, layer, cw, cb, dtb, aneg, dskip, e, sb):
    s = xbc.shape[0]
    row = lambda width: pl.BlockSpec((sb, width), lambda i: (i, 0))
    cs_w = cs.shape[1]
    return pl.pallas_call(
        _ssd_sample_kernel,
        grid=(s // sb,),
        in_specs=[row(CONV_DIM), row(DT_PAD), row(cs_w),
                  pl.BlockSpec((None, sb, SSM_INNER, SSM_STATE), lambda i: (layer, i, 0, 0))] +
                 [_const_spec(a.shape) for a in (cw, cb, dtb, aneg, dskip, e)],
        out_specs=(row(SSM_INNER),
                   pl.BlockSpec((sb, SSM_INNER, SSM_STATE), lambda i: (i, 0, 0)),
                   row(cs_w)),
        out_shape=(jax.ShapeDtypeStruct((s, SSM_INNER), F32),
                   jax.ShapeDtypeStruct((s, SSM_INNER, SSM_STATE), F32),
                   jax.ShapeDtypeStruct((s, cs_w), F32)),
        compiler_params=_cparams(("parallel",)),
        name="ssd_sample",
    )(xbc, dt, cs, state, cw, cb, dtb, aneg, dskip, e)


def _pad_lanes(v, width):
    return jnp.pad(v.astype(F32), (0, width - v.shape[0])).reshape(1, width)


def kernel(x_prompt, x_sample, cache_k, cache_v, state_ssm, state_conv, state_ffn_conv, page_table,
           meta_tokens, norm1_w, w_in, sb_logit_bias, conv_w, conv_b, dt_bias, a_log, d_skip, ssm_norm_w,
           w_sb_out, w_ssm_out, w_out, norm2_w, w_ffn_up, ffn_conv_w, ffn_conv_b, w_ffn_down, final_norm_w):
    b, s, d = x_prompt.shape
    depth = w_in.shape[0]
    n = s + BLOCK
    ns = x_sample.shape[0]
    f = ffn_conv_w.shape[2]
    assert x_sample.shape[1] == 1 and n % BLOCK == 0
    tm_p = 512
    tm_f = n // 8
    assert (b * n) % tm_p == 0 and tm_f % SUBLANES == 0

    hp = jnp.concatenate([jnp.zeros((b, PAD, d), F32),
                          jnp.broadcast_to(meta_tokens.astype(F32)[None], (b, N_META, d)),
                          x_prompt], axis=1).reshape(b * n, d)
    hs = x_sample.reshape(ns, d)

    hid = jnp.arange(SSM_INNER, dtype=jnp.int32) // SSM_HEAD_DIM
    e_mat = (jnp.arange(DT_PAD, dtype=jnp.int32)[:, None] == hid[None, :]).astype(F32)
    row_head = jnp.arange(DT_PAD, dtype=jnp.int32)
    e3_mat = ((row_head[:, None] % SSM_HEADS == hid[None, :]) &
              (row_head[:, None] < 3 * SSM_HEADS)).astype(BF16)
    state4 = state_ssm.reshape(depth, ns, SSM_INNER, SSM_STATE)
    pool, page = cache_k.shape[1], cache_k.shape[2]
    cache_kt = jnp.transpose(cache_k, (0, 1, 3, 4, 2)).reshape(depth, pool, SB_WIDTH, page)
    cache_vt = jnp.transpose(cache_v, (0, 1, 3, 4, 2)).reshape(depth, pool, SB_WIDTH, page)

    o_qkv = 3 * SB_WIDTH
    o_z = o_qkv
    o_xbc = o_z + SSM_INNER
    o_dt = o_xbc + CONV_DIM
    o_g = o_dt + SSM_HEADS

    outs = {k: [] for k in ("kp", "vp", "sp", "cp", "fp", "ks", "vs", "ss", "cs", "fs")}
    for l in range(depth):
        final = l == depth - 1
        wl = w_in[l]
        w_a = jnp.concatenate([wl[:, :o_qkv], wl[:, o_xbc:o_dt],
                               jnp.pad(wl[:, o_dt:o_g], ((0, 0), (0, DT_PAD - SSM_HEADS)))], axis=1).astype(BF16)
        wz = wl[:, o_z:o_xbc].astype(BF16)
        wg = wl[:, o_g:].astype(BF16)
        wsb = w_sb_out[l].astype(BF16)
        wssm = w_ssm_out[l].astype(BF16)
        wo = w_out[l].astype(BF16)
        wup = w_ffn_up[l].astype(BF16)
        wdn = w_ffn_down[l].astype(BF16)
        n1w = norm1_w[l].reshape(1, d)
        n2w = norm2_w[l].reshape(1, d)
        fw = final_norm_w.reshape(1, d)
        snw = ssm_norm_w[l].reshape(1, SSM_INNER)
        cw = conv_w[l]
        cb = conv_b[l].reshape(1, CONV_DIM)
        dtb = _pad_lanes(dt_bias[l], DT_PAD)
        aneg = _pad_lanes(-jnp.exp(a_log[l].astype(F32)), DT_PAD)
        dskip = jnp.repeat(d_skip[l].astype(F32), SSM_HEAD_DIM).reshape(1, SSM_INNER)
        fcw = ffn_conv_w[l]
        fcb = ffn_conv_b[l].reshape(1, f)
        bias = sb_logit_bias[l].astype(F32)

        q, kb, vb, k32, v32, xbc, dt, xn = _inproj(hp, n1w, w_a, tm_p)
        osb = _sb_attn_prompt(q.reshape(b, n, SB_WIDTH), kb.reshape(b, n, SB_WIDTH),
                              vb.reshape(b, n, SB_WIDTH), bias).reshape(b * n, SB_WIDTH)
        y, st, cst = _ssd_prompt(xbc, dt, b, cw, cb, dtb, aneg, dskip, e3_mat)
        h1, xn2 = _mix_out(xn, hp, osb, y, wz, wg, wsb, wssm, wo, snw, n2w, tm_p)
        hp, fst = _ffn_prompt(xn2, h1, b, wup, wdn, fcw, fcb, fw, tm_f, final)
        outs["kp"].append(k32.reshape(b, n, SB_HEADS, SB_HEAD_DIM)[:, PAD:])
        outs["vp"].append(v32.reshape(b, n, SB_HEADS, SB_HEAD_DIM)[:, PAD:])
        outs["sp"].append(st.reshape(b, SSM_HEADS, SSM_HEAD_DIM, SSM_STATE))
        outs["cp"].append(cst[:, SUBLANES - (SSM_CONV - 1):])
        outs["fp"].append(fst[:, SUBLANES - (FFN_CONV - 1):])

        q, kb, vb, k32, v32, xbc, dt, xn = _inproj(hs, n1w, w_a, ns)
        osb = _sb_attn_sample(q.astype(F32), bias, cache_kt, cache_vt, page_table, l).astype(BF16)
        cs_in = state_conv[l].reshape(ns, (SSM_CONV - 1) * CONV_DIM)
        y, nst, ncs = _ssd_sample(xbc, dt, cs_in, state4, l, cw, cb, dtb, aneg, dskip, e_mat, SUBLANES)
        h1, xn2 = _mix_out(xn, hs, osb, y, wz, wg, wsb, wssm, wo, snw, n2w, ns)
        fs_in = state_ffn_conv[l].reshape(ns, (FFN_CONV - 1) * f)
        hs, nfs = _ffn_sample(xn2, h1, fs_in, wup, wdn, fcw, fcb, fw, final)
        outs["ks"].append(k32.reshape(ns, 1, SB_HEADS, SB_HEAD_DIM))
        outs["vs"].append(v32.reshape(ns, 1, SB_HEADS, SB_HEAD_DIM))
        outs["ss"].append(nst.reshape(ns, SSM_HEADS, SSM_HEAD_DIM, SSM_STATE))
        outs["cs"].append(ncs.reshape(ns, SSM_CONV - 1, CONV_DIM))
        outs["fs"].append(nfs.reshape(ns, FFN_CONV - 1, f))

    y_prompt = hp.reshape(b, n, d)[:, BLOCK:]
    y_sample = hs.reshape(ns, 1, d)
    st = {k: jnp.stack(v) for k, v in outs.items()}
    return (y_prompt, y_sample, st["kp"], st["vp"], st["ks"], st["vs"],
            st["sp"], st["ss"], st["cp"], st["cs"], st["fp"], st["fs"])
```

```python
import functools

import jax
import jax.numpy as jnp
from jax import lax
from jax.experimental import pallas as pl
from jax.experimental.pallas import tpu as pltpu

F32 = jnp.float32
BF16 = jnp.bfloat16
HIGHEST = lax.Precision.HIGHEST

N_META = 16
BLOCK = 128
PAD = BLOCK - N_META
SB_HEADS = 8
SB_HEAD_DIM = 64
SB_WIDTH = SB_HEADS * SB_HEAD_DIM
SB_SCALE = SB_HEAD_DIM ** -0.5
SSM_HEADS = 16
SSM_HEAD_DIM = 64
SSM_INNER = SSM_HEADS * SSM_HEAD_DIM
SSM_GROUPS = 4
SSM_STATE = 128
SSM_CONV = 4
CONV_DIM = SSM_INNER + 2 * SSM_GROUPS * SSM_STATE
FFN_CONV = 3
EPS = 1e-6
LANES = 128
SUBLANES = 8
DT_PAD = LANES
ATTN_HEADS = 8
ATTN_UNROLL = 4
VMEM_LIMIT = 56 * 1024 * 1024


def _cparams(sem):
    return pltpu.CompilerParams(dimension_semantics=sem, vmem_limit_bytes=VMEM_LIMIT)


def _const_spec(shape):
    nd = len(shape)
    return pl.BlockSpec(shape, lambda *_: (0,) * nd, pipeline_mode=pl.Buffered(1))


def _full_spec(shape):
    nd = len(shape)
    return pl.BlockSpec(shape, lambda *_: (0,) * nd)


def _bdot(a, b):
    return jnp.dot(a, b, preferred_element_type=F32)


def _dot_nt(a, b):
    return lax.dot_general(a, b, (((1,), (1,)), ((), ())), preferred_element_type=F32)


def _dot_tn(a, b, precision=None):
    return lax.dot_general(a, b, (((0,), (0,)), ((), ())), preferred_element_type=F32, precision=precision)


def _hdot(a, b):
    return jnp.dot(a, b, preferred_element_type=F32, precision=HIGHEST)


def _sigmoid(x):
    return 1.0 / (1.0 + jnp.exp(-x))


def _silu(x):
    return x * _sigmoid(x)


def _softplus(x):
    return jnp.maximum(x, 0.0) + jnp.log(1.0 + jnp.exp(-jnp.abs(x)))


def _rms(x, w):
    return x * lax.rsqrt(jnp.mean(x * x, axis=-1, keepdims=True) + EPS) * w


def _split_bf16(x):
    hi = x.astype(BF16)
    lo = (x - hi.astype(F32)).astype(BF16)
    return hi, lo


def _inproj_kernel(h_ref, nw_ref, w_ref, q_ref, kb_ref, vb_ref, k_ref, v_ref, xbc_ref, dt_ref, xn_ref):
    xn = _rms(h_ref[...], nw_ref[...]).astype(BF16)
    xn_ref[...] = xn
    qkv = _bdot(xn, w_ref[:, 0:3 * SB_WIDTH])
    q_ref[...] = (qkv[:, 0:SB_WIDTH] * SB_SCALE).astype(BF16)
    k = qkv[:, SB_WIDTH:2 * SB_WIDTH]
    v = qkv[:, 2 * SB_WIDTH:3 * SB_WIDTH]
    k_ref[...] = k
    v_ref[...] = v
    kb_ref[...] = k.astype(BF16)
    vb_ref[...] = v.astype(BF16)
    o = 3 * SB_WIDTH
    xbc_ref[...] = _bdot(xn, w_ref[:, o:o + CONV_DIM])
    dt_ref[...] = _bdot(xn, w_ref[:, o + CONV_DIM:o + CONV_DIM + DT_PAD])


def _inproj(h, norm_w, w_a, tm):
    m, d = h.shape
    wa = w_a.shape[1]
    row = lambda width: pl.BlockSpec((tm, width), lambda i: (i, 0))
    out_shape = (
        jax.ShapeDtypeStruct((m, SB_WIDTH), BF16),
        jax.ShapeDtypeStruct((m, SB_WIDTH), BF16),
        jax.ShapeDtypeStruct((m, SB_WIDTH), BF16),
        jax.ShapeDtypeStruct((m, SB_WIDTH), F32),
        jax.ShapeDtypeStruct((m, SB_WIDTH), F32),
        jax.ShapeDtypeStruct((m, CONV_DIM), F32),
        jax.ShapeDtypeStruct((m, DT_PAD), F32),
        jax.ShapeDtypeStruct((m, d), BF16),
    )
    return pl.pallas_call(
        _inproj_kernel,
        grid=(m // tm,),
        in_specs=[row(d), _const_spec((1, d)), _const_spec((d, wa))],
        out_specs=(row(SB_WIDTH), row(SB_WIDTH), row(SB_WIDTH), row(SB_WIDTH), row(SB_WIDTH),
                   row(CONV_DIM), row(DT_PAD), row(d)),
        out_shape=out_shape,
        compiler_params=_cparams(("parallel",)),
        name="inproj",
    )(h, norm_w, w_a)


def _cumsum_matrix():
    r = lax.broadcasted_iota(jnp.int32, (2 * BLOCK, 2 * BLOCK), 0) % BLOCK
    c = lax.broadcasted_iota(jnp.int32, (2 * BLOCK, 2 * BLOCK), 1)
    return jnp.where((r > c) | (c >= BLOCK), 1.0, 0.0).astype(BF16)


def _attn_kernel(bias_ref, q_ref, k_ref, v_ref, o_ref, acc_ref, carry_ref):
    hg = pl.program_id(1)
    qi = pl.program_id(2)
    lane = lax.broadcasted_iota(jnp.int32, (BLOCK, BLOCK), 1)
    uu = _cumsum_matrix()
    q_heads, biases = [], []
    for h in range(ATTN_HEADS):
        q = q_ref[:, (h // 2) * LANES:(h // 2 + 1) * LANES]
        keep = (lane < SB_HEAD_DIM) if h % 2 == 0 else (lane >= SB_HEAD_DIM)
        q_heads.append(jnp.where(keep, q, jnp.zeros_like(q)))
        biases.append(bias_ref[hg * ATTN_HEADS + h])

    def process(blk, nblk, masked):
        width = nblk * BLOCK
        start = blk * BLOCK if isinstance(blk, int) else pl.multiple_of(blk * BLOCK, BLOCK)
        if masked:
            t_pos = qi * BLOCK + lax.broadcasted_iota(jnp.int32, (BLOCK, width), 0)
            s_pos = start + lax.broadcasted_iota(jnp.int32, (BLOCK, width), 1)
            mask = (s_pos < t_pos) & (s_pos >= PAD)
        for h in range(ATTN_HEADS):
            cols = slice((h // 2) * LANES, (h // 2 + 1) * LANES)
            z = _dot_nt(q_heads[h], k_ref[pl.ds(start, width), cols]) + biases[h]
            lk = -_softplus(z)
            if masked:
                lk = jnp.where(mask, lk, 0.0)
            hi, lo = _split_bf16(lk)
            carry = carry_ref[h]
            between = [None] * nblk
            for j in reversed(range(nblk)):
                sl = slice(j * BLOCK, (j + 1) * BLOCK)
                r = _bdot(jnp.concatenate([hi[:, sl], lo[:, sl]], axis=1), uu)
                between[j] = r[:, :BLOCK] + carry
                carry = carry + r[:, BLOCK:]
            carry_ref[h] = carry
            w = jnp.exp(z + lk + jnp.concatenate(between, axis=1))
            if masked:
                w = jnp.where(mask, w, 0.0)
            acc_ref[h] += _bdot(w.astype(BF16), v_ref[pl.ds(start, width), cols])

    acc_ref[...] = jnp.zeros_like(acc_ref)
    carry_ref[...] = jnp.zeros_like(carry_ref)
    process(qi, 1, True)
    n_inner = jnp.maximum(qi - 1, 0)
    n_full = n_inner // ATTN_UNROLL

    def body(i, _):
        process(qi - ATTN_UNROLL * (i + 1), ATTN_UNROLL, False)
        return 0

    lax.fori_loop(0, n_full, body, 0)
    rem = n_inner - ATTN_UNROLL * n_full
    for r in range(ATTN_UNROLL):
        pl.when((qi > 0) & (rem == r))(functools.partial(process, 0, r + 1, True))

    for p in range(ATTN_HEADS // 2):
        o_ref[:, p * LANES:(p + 1) * LANES] = jnp.where(
            lane < SB_HEAD_DIM, acc_ref[2 * p], acc_ref[2 * p + 1]).astype(o_ref.dtype)


def _sb_attn_prompt(q, k, v, bias):
    b, n, _ = q.shape
    nb = n // BLOCK
    gw = ATTN_HEADS * SB_HEAD_DIM
    return pl.pallas_call(
        _attn_kernel,
        grid=(b, SB_WIDTH // gw, nb),
        in_specs=[
            pl.BlockSpec(memory_space=pltpu.SMEM),
            pl.BlockSpec((None, BLOCK, gw), lambda bi, p, qi: (bi, qi, p)),
            pl.BlockSpec((None, n, gw), lambda bi, p, qi: (bi, 0, p)),
            pl.BlockSpec((None, n, gw), lambda bi, p, qi: (bi, 0, p)),
        ],
        out_specs=pl.BlockSpec((None, BLOCK, gw), lambda bi, p, qi: (bi, qi, p)),
        scratch_shapes=[pltpu.VMEM((ATTN_HEADS, BLOCK, BLOCK), F32), pltpu.VMEM((ATTN_HEADS, BLOCK, BLOCK), F32)],
        out_shape=jax.ShapeDtypeStruct((b, n, SB_WIDTH), BF16),
        compiler_params=_cparams(("parallel", "parallel", "arbitrary")),
        name="sb_attn_prompt",
    )(bias, q, k, v)


def _shift_rows(x, prev_tail, k):
    rolled = pltpu.roll(x, k, axis=0)
    fix = pltpu.roll(prev_tail, k, axis=0)
    row = lax.broadcasted_iota(jnp.int32, prev_tail.shape, 0)
    head = jnp.where(row < k, fix, rolled[:SUBLANES])
    return jnp.concatenate([head, rolled[SUBLANES:]], axis=0)


def _expand_heads(x, e3):
    lane = lax.broadcasted_iota(jnp.int32, x.shape, 1)
    x = jnp.where(lane < SSM_HEADS, x, 0.0)
    hi = x.astype(BF16).astype(F32)
    r1 = x - hi
    mid = r1.astype(BF16).astype(F32)
    lo = r1 - mid
    packed = hi + pltpu.roll(mid, SSM_HEADS, axis=1) + pltpu.roll(lo, 2 * SSM_HEADS, axis=1)
    return _bdot(packed.astype(BF16), e3)


def _ssd_kernel(xbc_ref, dt_ref, cw_ref, cb_ref, dtb_ref, aneg_ref, dskip_ref, e3_ref,
                y_ref, st_ref, cs_ref, hist_ref):
    c = pl.program_id(1)
    row = lax.broadcasted_iota(jnp.int32, (BLOCK, 1), 0)
    valid = jnp.where((c > 0) | (row >= PAD), 1.0, 0.0)

    @pl.when(c == 0)
    def _():
        hist_ref[...] = jnp.zeros_like(hist_ref)
        st_ref[...] = jnp.zeros_like(st_ref)

    x = xbc_ref[...] * valid
    prev_tail = hist_ref[...]
    conv = cb_ref[...] + x * cw_ref[SSM_CONV - 1:SSM_CONV, :]
    for j in range(SSM_CONV - 1):
        conv = conv + _shift_rows(x, prev_tail, SSM_CONV - 1 - j) * cw_ref[j:j + 1, :]
    tail = x[BLOCK - SUBLANES:, :]
    hist_ref[...] = tail
    cs_ref[...] = tail
    xc = _silu(conv)
    xs = xc[:, :SSM_INNER]
    bm = xc[:, SSM_INNER:SSM_INNER + SSM_GROUPS * SSM_STATE].astype(BF16)
    cm = xc[:, SSM_INNER + SSM_GROUPS * SSM_STATE:].astype(BF16)

    dt = _softplus(dt_ref[...] + dtb_ref[...]) * valid
    la = dt * aneg_ref[...]
    ri = lax.broadcasted_iota(jnp.int32, (BLOCK, BLOCK), 0)
    ci = lax.broadcasted_iota(jnp.int32, (BLOCK, BLOCK), 1)
    causal = ci <= ri
    tri = jnp.where(causal, 1.0, 0.0)
    cum = _hdot(tri, la)
    cum_t = cum.T
    e3 = e3_ref[...]
    xdt = xs * _expand_heads(dt, e3)
    ecum = _expand_heads(jnp.exp(cum), e3)
    cum_last = cum[BLOCK - 1:BLOCK, :]
    xdtw = (xdt * _expand_heads(jnp.exp(cum_last - cum), e3)).astype(BF16)
    xdt_b = xdt.astype(BF16)
    ecl = jnp.exp(cum_last)
    lane = ci

    for g in range(SSM_GROUPS):
        cmg = cm[:, g * SSM_STATE:(g + 1) * SSM_STATE]
        bmg = bm[:, g * SSM_STATE:(g + 1) * SSM_STATE]
        cb = _dot_nt(cmg, bmg)
        for pp in range(2):
            pair = 2 * g + pp
            cols = slice(pair * LANES, (pair + 1) * LANES)
            ys = []
            for hh in range(2):
                h = 2 * pair + hh
                seg = cum[:, h:h + 1] - cum_t[h:h + 1, :]
                m = (cb * jnp.where(causal, jnp.exp(seg), 0.0)).astype(BF16)
                ys.append(_bdot(m, xdt_b[:, cols]))
            y = jnp.where(lane < SSM_HEAD_DIM, ys[0], ys[1])
            st = st_ref[cols, :]
            y = y + _dot_nt(cmg, st.astype(BF16)) * ecum[:, cols]
            fac = jnp.where(ri < SSM_HEAD_DIM,
                            jnp.broadcast_to(ecl[:, 2 * pair:2 * pair + 1], (LANES, SSM_STATE)),
                            jnp.broadcast_to(ecl[:, 2 * pair + 1:2 * pair + 2], (LANES, SSM_STATE)))
            st_ref[cols, :] = st * fac + _dot_tn(xdtw[:, cols], bmg)
            y_ref[:, cols] = y + xs[:, cols] * dskip_ref[:, cols]


def _ssd_prompt(xbc, dt, b, cw, cb, dtb, aneg, dskip, e3):
    m = xbc.shape[0]
    nc = m // b // BLOCK
    row = lambda width: pl.BlockSpec((BLOCK, width), lambda bi, c: (bi * nc + c, 0))
    out_shape = (
        jax.ShapeDtypeStruct((m, SSM_INNER), F32),
        jax.ShapeDtypeStruct((b, SSM_INNER, SSM_STATE), F32),
        jax.ShapeDtypeStruct((b, SUBLANES, CONV_DIM), F32),
    )
    return pl.pallas_call(
        _ssd_kernel,
        grid=(b, nc),
        in_specs=[row(CONV_DIM), row(DT_PAD), _const_spec(cw.shape), _const_spec(cb.shape),
                  _const_spec(dtb.shape), _const_spec(aneg.shape), _const_spec(dskip.shape),
                  _const_spec(e3.shape)],
        out_specs=(row(SSM_INNER),
                   pl.BlockSpec((None, SSM_INNER, SSM_STATE), lambda bi, c: (bi, 0, 0)),
                   pl.BlockSpec((None, SUBLANES, CONV_DIM), lambda bi, c: (bi, 0, 0))),
        out_shape=out_shape,
        scratch_shapes=[pltpu.VMEM((SUBLANES, CONV_DIM), F32)],
        compiler_params=_cparams(("parallel", "arbitrary")),
        name="ssd_prompt",
    )(xbc, dt, cw, cb, dtb, aneg, dskip, e3)


def _mix_kernel(xn_ref, h_ref, osb_ref, y_ref, wz_ref, wg_ref, wsb_ref, wssm_ref, wo_ref, snw_ref, n2w_ref,
                h1_ref, xn2_ref):
    xn = xn_ref[...]
    d = h_ref.shape[1]
    z = _bdot(xn, wz_ref[...])
    yz = y_ref[...] * _silu(z)
    gw = SSM_INNER // SSM_GROUPS
    parts = []
    for g in range(SSM_GROUPS):
        yg = yz[:, g * gw:(g + 1) * gw]
        parts.append(yg * lax.rsqrt(jnp.mean(yg * yg, axis=-1, keepdims=True) + EPS))
    yn = (jnp.concatenate(parts, axis=1) * snw_ref[...]).astype(BF16)
    g_a = _sigmoid(_bdot(xn, wg_ref[:, 0:d]))
    merged = g_a * _bdot(osb_ref[...], wsb_ref[...])
    g_s = _sigmoid(_bdot(xn, wg_ref[:, d:2 * d]))
    merged = merged + g_s * _bdot(yn, wssm_ref[...])
    h1 = h_ref[...] + _bdot(merged.astype(BF16), wo_ref[...])
    h1_ref[...] = h1
    xn2_ref[...] = _rms(h1, n2w_ref[...]).astype(BF16)


def _mix_out(xn, h, osb, y, wz, wg, wsb, wssm, wo, snw, n2w, tm):
    m, d = h.shape
    row = lambda width: pl.BlockSpec((tm, width), lambda i: (i, 0))
    return pl.pallas_call(
        _mix_kernel,
        grid=(m // tm,),
        in_specs=[row(d), row(d), row(SB_WIDTH), row(SSM_INNER)] +
                 [_const_spec(a.shape) for a in (wz, wg, wsb, wssm, wo, snw, n2w)],
        out_specs=(row(d), row(d)),
        out_shape=(jax.ShapeDtypeStruct((m, d), F32), jax.ShapeDtypeStruct((m, d), BF16)),
        compiler_params=_cparams(("parallel",)),
        name="mix_out",
    )(xn, h, osb, y, wz, wg, wsb, wssm, wo, snw, n2w)


def _ffn_tail(hdn, h1, wdn_ref, fw_ref, out_ref, final):
    h2 = h1 + _bdot(hdn.astype(BF16), wdn_ref[...])
    out_ref[...] = _rms(h2, fw_ref[...]) if final else h2


def _ffn_prompt_kernel(xn2_ref, h1_ref, wup_ref, wdn_ref, cw_ref, cb_ref, fw_ref, out_ref, fs_ref, hist_ref,
                       *, final):
    t = pl.program_id(1)
    tm = xn2_ref.shape[0]
    f = cw_ref.shape[1]

    @pl.when(t == 0)
    def _():
        hist_ref[0:SUBLANES, :] = jnp.zeros((SUBLANES, f), F32)

    row = lax.broadcasted_iota(jnp.int32, (tm, 1), 0)
    valid = jnp.where((t > 0) | (row >= PAD), 1.0, 0.0)
    xn2 = xn2_ref[...]
    gate = _bdot(xn2, wup_ref[:, 0:f]) * valid
    hist_ref[SUBLANES:SUBLANES + tm, :] = gate
    conv = cb_ref[...] + gate * cw_ref[FFN_CONV - 1:FFN_CONV, :]
    for j in range(FFN_CONV - 1):
        back = FFN_CONV - 1 - j
        conv = conv + hist_ref[SUBLANES - back:SUBLANES - back + tm, :] * cw_ref[j:j + 1, :]
    tail = gate[tm - SUBLANES:, :]
    hist_ref[0:SUBLANES, :] = tail
    fs_ref[...] = tail
    hdn = _silu(conv) * _bdot(xn2, wup_ref[:, f:2 * f])
    _ffn_tail(hdn, h1_ref[...], wdn_ref, fw_ref, out_ref, final)


def _ffn_prompt(xn2, h1, b, wup, wdn, cw, cb, fw, tm, final):
    m, d = h1.shape
    f = cw.shape[1]
    nt = m // b // tm
    row = lambda width: pl.BlockSpec((tm, width), lambda bi, t: (bi * nt + t, 0))
    return pl.pallas_call(
        functools.partial(_ffn_prompt_kernel, final=final),
        grid=(b, nt),
        in_specs=[row(d), row(d)] + [_const_spec(a.shape) for a in (wup, wdn, cw, cb, fw)],
        out_specs=(row(d), pl.BlockSpec((None, SUBLANES, f), lambda bi, t: (bi, 0, 0))),
        out_shape=(jax.ShapeDtypeStruct((m, d), F32), jax.ShapeDtypeStruct((b, SUBLANES, f), F32)),
        scratch_shapes=[pltpu.VMEM((SUBLANES + tm, f), F32)],
        compiler_params=_cparams(("parallel", "arbitrary")),
        name="ffn_prompt",
    )(xn2, h1, wup, wdn, cw, cb, fw)


def _ffn_sample_kernel(xn2_ref, h1_ref, hist_ref, wup_ref, wdn_ref, cw_ref, cb_ref, fw_ref, out_ref, fs_ref,
                       *, final):
    f = cw_ref.shape[1]
    xn2 = xn2_ref[...]
    gate = _bdot(xn2, wup_ref[:, 0:f])
    conv = cb_ref[...] + gate * cw_ref[FFN_CONV - 1:FFN_CONV, :]
    for j in range(FFN_CONV - 1):
        conv = conv + hist_ref[:, j * f:(j + 1) * f] * cw_ref[j:j + 1, :]
    for j in range(FFN_CONV - 2):
        fs_ref[:, j * f:(j + 1) * f] = hist_ref[:, (j + 1) * f:(j + 2) * f]
    fs_ref[:, (FFN_CONV - 2) * f:(FFN_CONV - 1) * f] = gate
    hdn = _silu(conv) * _bdot(xn2, wup_ref[:, f:2 * f])
    _ffn_tail(hdn, h1_ref[...], wdn_ref, fw_ref, out_ref, final)


def _ffn_sample(xn2, h1, hist, wup, wdn, cw, cb, fw, final):
    m, d = h1.shape
    return pl.pallas_call(
        functools.partial(_ffn_sample_kernel, final=final),
        grid=(1,),
        in_specs=[_const_spec(a.shape) for a in (xn2, h1, hist, wup, wdn, cw, cb, fw)],
        out_specs=(_full_spec((m, d)), _full_spec(hist.shape)),
        out_shape=(jax.ShapeDtypeStruct((m, d), F32), jax.ShapeDtypeStruct(hist.shape, F32)),
        compiler_params=_cparams(("arbitrary",)),
        name="ffn_sample",
    )(xn2, h1, hist, wup, wdn, cw, cb, fw)


def _attn_sample_kernel(pt_ref, q_ref, bias_ref, *refs, n_pages):
    k_refs = refs[:n_pages]
    v_refs = refs[n_pages:2 * n_pages]
    o_ref = refs[2 * n_pages]
    hrow = lax.broadcasted_iota(jnp.int32, (SB_HEADS, SB_WIDTH), 0)
    hcol = lax.broadcasted_iota(jnp.int32, (SB_HEADS, SB_WIDTH), 1) // SB_HEAD_DIM
    head_mask = hrow == hcol
    qm = jnp.where(head_mask, jnp.broadcast_to(q_ref[...], (SB_HEADS, SB_WIDTH)), 0.0).astype(BF16)
    bias = bias_ref[...]
    z = jnp.concatenate([_bdot(qm, k_refs[c][...].astype(BF16)) + bias for c in range(n_pages)], axis=0)
    lk = -_softplus(z)
    nr = n_pages * SB_HEADS
    hi, lo = _split_bf16(lk)
    r = _bdot(jnp.concatenate([hi, lo], axis=1), _cumsum_matrix())
    ra = lax.broadcasted_iota(jnp.int32, (nr, nr), 0)
    rb = lax.broadcasted_iota(jnp.int32, (nr, nr), 1)
    mx = jnp.where((ra % SB_HEADS == rb % SB_HEADS) & (rb // SB_HEADS > ra // SB_HEADS), 1.0, 0.0).astype(BF16)
    thi, tlo = _split_bf16(r[:, BLOCK:])
    lt = _bdot(mx, jnp.concatenate([thi, tlo], axis=1))
    w = jnp.exp(z + lk + r[:, :BLOCK] + lt[:, :BLOCK] + lt[:, BLOCK:])
    o_full = jnp.zeros((SB_HEADS, SB_WIDTH), F32)
    for c in range(n_pages):
        wc = w[c * SB_HEADS:(c + 1) * SB_HEADS, :].astype(BF16)
        o_full = o_full + _dot_nt(wc, v_refs[c][...].astype(BF16))
    o_ref[...] = jnp.sum(jnp.where(head_mask, o_full, 0.0), axis=0, keepdims=True)


def _sb_attn_sample(q, bias, cache_kt, cache_vt, page_table, layer):
    s = q.shape[0]
    n_pages = page_table.shape[1]
    page = cache_kt.shape[3]
    assert page == BLOCK and n_pages * SB_HEADS == BLOCK
    q3 = q.reshape(s, 1, SB_WIDTH)
    bias_b = jnp.broadcast_to(bias[:, None], (SB_HEADS, BLOCK))

    def page_spec(c):
        return pl.BlockSpec((None, None, SB_WIDTH, page), lambda i, pt, c=c: (layer, pt[i * n_pages + c], 0, 0))

    grid_spec = pltpu.PrefetchScalarGridSpec(
        num_scalar_prefetch=1,
        grid=(s,),
        in_specs=[pl.BlockSpec((None, 1, SB_WIDTH), lambda i, pt: (i, 0, 0)),
                  pl.BlockSpec((SB_HEADS, BLOCK), lambda i, pt: (0, 0))] +
                 [page_spec(c) for c in range(n_pages)] * 2,
        out_specs=pl.BlockSpec((None, 1, SB_WIDTH), lambda i, pt: (i, 0, 0)),
    )
    out = pl.pallas_call(
        functools.partial(_attn_sample_kernel, n_pages=n_pages),
        grid_spec=grid_spec,
        out_shape=jax.ShapeDtypeStruct((s, 1, SB_WIDTH), F32),
        compiler_params=_cparams(("arbitrary",)),
        name="sb_attn_sample",
    )(page_table.reshape(-1), q3, bias_b, *([cache_kt] * n_pages), *([cache_vt] * n_pages))
    return out.reshape(s, SB_WIDTH)


def _ssd_sample_kernel(xbc_ref, dt_ref, cs_ref, st_ref, cw_ref, cb_ref, dtb_ref, aneg_ref, dskip_ref, e_ref,
                       y_ref, nst_ref, ncs_ref):
    sb = xbc_ref.shape[0]
    x = xbc_ref[...]
    conv = cb_ref[...] + x * cw_ref[SSM_CONV - 1:SSM_CONV, :]
    for j in range(SSM_CONV - 1):
        conv = conv + cs_ref[:, j * CONV_DIM:(j + 1) * CONV_DIM] * cw_ref[j:j + 1, :]
    for j in range(SSM_CONV - 2):
        ncs_ref[:, j * CONV_DIM:(j + 1) * CONV_DIM] = cs_ref[:, (j + 1) * CONV_DIM:(j + 2) * CONV_DIM]
    ncs_ref[:, (SSM_CONV - 2) * CONV_DIM:(SSM_CONV - 1) * CONV_DIM] = x
    xc = _silu(conv)
    xs = xc[:, :SSM_INNER]
    bm = xc[:, SSM_INNER:SSM_INNER + SSM_GROUPS * SSM_STATE]
    cm = xc[:, SSM_INNER + SSM_GROUPS * SSM_STATE:]
    dt = _softplus(dt_ref[...] + dtb_ref[...])
    e = e_ref[...]
    xdt = xs * _hdot(dt, e)
    dec = _hdot(jnp.exp(dt * aneg_ref[...]), e)
    gw = SSM_INNER // SSM_GROUPS
    cb_parts = []
    for g in range(SSM_GROUPS):
        sl = slice(g * SSM_STATE, (g + 1) * SSM_STATE)
        cbg = jnp.sum(cm[:, sl] * bm[:, sl], axis=-1, keepdims=True)
        cb_parts.append(jnp.broadcast_to(cbg, (sb, gw)))
    y = jnp.concatenate(cb_parts, axis=1) * xdt + xs * dskip_ref[...]

    grow = lax.broadcasted_iota(jnp.int32, (SUBLANES, SSM_INNER), 0)
    gcol = lax.broadcasted_iota(jnp.int32, (SUBLANES, SSM_INNER), 1) // gw
    gmask = grow == gcol
    r8 = lax.broadcasted_iota(jnp.int32, (SUBLANES, SSM_STATE), 0)
    row8 = lax.broadcasted_iota(jnp.int32, (SUBLANES, SSM_INNER), 0)
    rsb = lax.broadcasted_iota(jnp.int32, (sb, SSM_INNER), 0)
    ystate = jnp.zeros((sb, SSM_INNER), F32)
    dec_hi = dec.astype(BF16).astype(F32)
    dec_mid = (dec - dec_hi).astype(BF16).astype(F32)
    dec_lo = dec - dec_hi - dec_mid
    x_hi = xdt.astype(BF16).astype(F32)
    x_lo = (xdt - x_hi).astype(BF16).astype(F32)
    b_hi = bm.astype(BF16).astype(F32)
    b_lo = (bm - b_hi).astype(BF16).astype(F32)
    ones3 = jnp.concatenate([jnp.where(r8 < 3, 1.0, 0.0), jnp.zeros((SUBLANES, SSM_STATE), F32)], axis=0).astype(BF16)
    zeros8 = jnp.zeros((SUBLANES, SSM_INNER), F32)

    def bcast8(v, i):
        return jnp.broadcast_to(v[i:i + 1, :], (SUBLANES, SSM_INNER))

    def group_rows(v, i):
        out = jnp.zeros((SUBLANES, SSM_STATE), F32)
        for g in range(SSM_GROUPS):
            vg = jnp.broadcast_to(v[i:i + 1, g * SSM_STATE:(g + 1) * SSM_STATE], (SUBLANES, SSM_STATE))
            out = jnp.where(r8 == g, vg, out)
        return out

    for i in range(sb):
        st = st_ref[i]
        y8 = _dot_nt(group_rows(cm, i).astype(BF16), st.astype(BF16))
        ys = jnp.sum(jnp.where(gmask, y8, 0.0), axis=0, keepdims=True)
        ystate = jnp.where(rsb == i, jnp.broadcast_to(ys, (sb, SSM_INNER)), ystate)
        d8 = jnp.where(row8 == 0, bcast8(dec_hi, i),
                       jnp.where(row8 == 1, bcast8(dec_mid, i), jnp.where(row8 == 2, bcast8(dec_lo, i), 0.0)))
        fac = _dot_tn(jnp.concatenate([d8, zeros8], axis=0).astype(BF16), ones3)
        xh8 = jnp.where(gmask, bcast8(x_hi, i), 0.0)
        xl8 = jnp.where(gmask, bcast8(x_lo, i), 0.0)
        bh8 = group_rows(b_hi, i)
        x16 = jnp.concatenate([xh8 + pltpu.roll(xh8, SSM_GROUPS, axis=0), xl8], axis=0).astype(BF16)
        b16 = jnp.concatenate([bh8 + pltpu.roll(group_rows(b_lo, i), SSM_GROUPS, axis=0), bh8], axis=0).astype(BF16)
        nst_ref[i] = st * fac + _dot_tn(x16, b16)
    y_ref[...] = y + ystate * dec


def _ssd_sample(xbc, dt, cs, state, layer, cw, cb, dtb, aneg, dskip, e, sb):
    s = xbc.shape[0]
    row = lambda width: pl.BlockSpec((sb, width), lambda i: (i, 0))
    cs_w = cs.shape[1]
    return pl.pallas_call(
        _ssd_sample_kernel,
        grid=(s // sb,),
        in_specs=[row(CONV_DIM), row(DT_PAD), row(cs_w),
                  pl.BlockSpec((None, sb, SSM_INNER, SSM_STATE), lambda i: (layer, i, 0, 0))] +
                 [_const_spec(a.shape) for a in (cw, cb, dtb, aneg, dskip, e)],
        out_specs=(row(SSM_INNER),
                   pl.BlockSpec((sb, SSM_INNER, SSM_STATE), lambda i: (i, 0, 0)),
                   row(cs_w)),
        out_shape=(jax.ShapeDtypeStruct((s, SSM_INNER), F32),
                   jax.ShapeDtypeStruct((s, SSM_INNER, SSM_STATE), F32),
                   jax.ShapeDtypeStruct((s, cs_w), F32)),
        compiler_params=_cparams(("parallel",)),
        name="ssd_sample",
    )(xbc, dt, cs, state, cw, cb, dtb, aneg, dskip, e)


def _pad_lanes(v, width):
    return jnp.pad(v.astype(F32), (0, width - v.shape[0])).reshape(1, width)


def kernel(x_prompt, x_sample, cache_k, cache_v, state_ssm, state_conv, state_ffn_conv, page_table,
           meta_tokens, norm1_w, w_in, sb_logit_bias, conv_w, conv_b, dt_bias, a_log, d_skip, ssm_norm_w,
           w_sb_out, w_ssm_out, w_out, norm2_w, w_ffn_up, ffn_conv_w, ffn_conv_b, w_ffn_down, final_norm_w):
    b, s, d = x_prompt.shape
    depth = w_in.shape[0]
    n = s + BLOCK
    ns = x_sample.shape[0]
    f = ffn_conv_w.shape[2]
    assert x_sample.shape[1] == 1 and n % BLOCK == 0
    tm_p = 512
    tm_f = n // 8
    assert (b * n) % tm_p == 0 and tm_f % SUBLANES == 0

    hp = jnp.concatenate([jnp.zeros((b, PAD, d), F32),
                          jnp.broadcast_to(meta_tokens.astype(F32)[None], (b, N_META, d)),
                          x_prompt], axis=1).reshape(b * n, d)
    hs = x_sample.reshape(ns, d)

    hid = jnp.arange(SSM_INNER, dtype=jnp.int32) // SSM_HEAD_DIM
    e_mat = (jnp.arange(DT_PAD, dtype=jnp.int32)[:, None] == hid[None, :]).astype(F32)
    row_head = jnp.arange(DT_PAD, dtype=jnp.int32)
    e3_mat = ((row_head[:, None] % SSM_HEADS == hid[None, :]) &
              (row_head[:, None] < 3 * SSM_HEADS)).astype(BF16)
    state4 = state_ssm.reshape(depth, ns, SSM_INNER, SSM_STATE)
    pool, page = cache_k.shape[1], cache_k.shape[2]
    cache_kt = jnp.transpose(cache_k, (0, 1, 3, 4, 2)).reshape(depth, pool, SB_WIDTH, page)
    cache_vt = jnp.transpose(cache_v, (0, 1, 3, 4, 2)).reshape(depth, pool, SB_WIDTH, page)

    o_qkv = 3 * SB_WIDTH
    o_z = o_qkv
    o_xbc = o_z + SSM_INNER
    o_dt = o_xbc + CONV_DIM
    o_g = o_dt + SSM_HEADS

    outs = {k: [] for k in ("kp", "vp", "sp", "cp", "fp", "ks", "vs", "ss", "cs", "fs")}
    for l in range(depth):
        final = l == depth - 1
        wl = w_in[l]
        w_a = jnp.concatenate([wl[:, :o_qkv], wl[:, o_xbc:o_dt],
                               jnp.pad(wl[:, o_dt:o_g], ((0, 0), (0, DT_PAD - SSM_HEADS)))], axis=1).astype(BF16)
        wz = wl[:, o_z:o_xbc].astype(BF16)
        wg = wl[:, o_g:].astype(BF16)
        wsb = w_sb_out[l].astype(BF16)
        wssm = w_ssm_out[l].astype(BF16)
        wo = w_out[l].astype(BF16)
        wup = w_ffn_up[l].astype(BF16)
        wdn = w_ffn_down[l].astype(BF16)
        n1w = norm1_w[l].reshape(1, d)
        n2w = norm2_w[l].reshape(1, d)
        fw = final_norm_w.reshape(1, d)
        snw = ssm_norm_w[l].reshape(1, SSM_INNER)
        cw = conv_w[l]
        cb = conv_b[l].reshape(1, CONV_DIM)
        dtb = _pad_lanes(dt_bias[l], DT_PAD)
        aneg = _pad_lanes(-jnp.exp(a_log[l].astype(F32)), DT_PAD)
        dskip = jnp.repeat(d_skip[l].astype(F32), SSM_HEAD_DIM).reshape(1, SSM_INNER)
        fcw = ffn_conv_w[l]
        fcb = ffn_conv_b[l].reshape(1, f)
        bias = sb_logit_bias[l].astype(F32)

        q, kb, vb, k32, v32, xbc, dt, xn = _inproj(hp, n1w, w_a, tm_p)
        osb = _sb_attn_prompt(q.reshape(b, n, SB_WIDTH), kb.reshape(b, n, SB_WIDTH),
                              vb.reshape(b, n, SB_WIDTH), bias).reshape(b * n, SB_WIDTH)
        y, st, cst = _ssd_prompt(xbc, dt, b, cw, cb, dtb, aneg, dskip, e3_mat)
        h1, xn2 = _mix_out(xn, hp, osb, y, wz, wg, wsb, wssm, wo, snw, n2w, tm_p)
        hp, fst = _ffn_prompt(xn2, h1, b, wup, wdn, fcw, fcb, fw, tm_f, final)
        outs["kp"].append(k32.reshape(b, n, SB_HEADS, SB_HEAD_DIM)[:, PAD:])
        outs["vp"].append(v32.reshape(b, n, SB_HEADS, SB_HEAD_DIM)[:, PAD:])
        outs["sp"].append(st.reshape(b, SSM_HEADS, SSM_HEAD_DIM, SSM_STATE))
        outs["cp"].append(cst[:, SUBLANES - (SSM_CONV - 1):])
        outs["fp"].append(fst[:, SUBLANES - (FFN_CONV - 1):])

        q, kb, vb, k32, v32, xbc, dt, xn = _inproj(hs, n1w, w_a, ns)
        osb = _sb_attn_sample(q.astype(F32), bias, cache_kt, cache_vt, page_table, l).astype(BF16)
        cs_in = state_conv[l].reshape(ns, (SSM_CONV - 1) * CONV_DIM)
        y, nst, ncs = _ssd_sample(xbc, dt, cs_in, state4, l, cw, cb, dtb, aneg, dskip, e_mat, SUBLANES)
        h1, xn2 = _mix_out(xn, hs, osb, y, wz, wg, wsb, wssm, wo, snw, n2w, ns)
        fs_in = state_ffn_conv[l].reshape(ns, (FFN_CONV - 1) * f)
        hs, nfs = _ffn_sample(xn2, h1, fs_in, wup, wdn, fcw, fcb, fw, final)
        outs["ks"].append(k32.reshape(ns, 1, SB_HEADS, SB_HEAD_DIM))
        outs["vs"].append(v32.reshape(ns, 1, SB_HEADS, SB_HEAD_DIM))
        outs["ss"].append(nst.reshape(ns, SSM_HEADS, SSM_HEAD_DIM, SSM_STATE))
        outs["cs"].append(ncs.reshape(ns, SSM_CONV - 1, CONV_DIM))
        outs["fs"].append(nfs.reshape(ns, FFN_CONV - 1, f))

    y_prompt = hp.reshape(b, n, d)[:, BLOCK:]
    y_sample = hs.reshape(ns, 1, d)
    st = {k: jnp.stack(v) for k, v in outs.items()}
    return (y_prompt, y_sample, st["kp"], st["vp"], st["ks"], st["vs"],
            st["sp"], st["ss"], st["cp"], st["cs"], st["fp"], st["fs"])
```

```python
import functools

import jax
import jax.numpy as jnp
from jax import lax
from jax.experimental import pallas as pl
from jax.experimental.pallas import tpu as pltpu

F32 = jnp.float32
BF16 = jnp.bfloat16
HIGHEST = lax.Precision.HIGHEST

N_META = 16
BLOCK = 128
PAD = BLOCK - N_META
SB_HEADS = 8
SB_HEAD_DIM = 64
SB_WIDTH = SB_HEADS * SB_HEAD_DIM
SB_SCALE = SB_HEAD_DIM ** -0.5
SSM_HEADS = 16
SSM_HEAD_DIM = 64
SSM_INNER = SSM_HEADS * SSM_HEAD_DIM
SSM_GROUPS = 4
SSM_STATE = 128
SSM_CONV = 4
CONV_DIM = SSM_INNER + 2 * SSM_GROUPS * SSM_STATE
FFN_CONV = 3
EPS = 1e-6
LANES = 128
SUBLANES = 8
DT_PAD = LANES
ATTN_HEADS = 8
ATTN_UNROLL = 4
VMEM_LIMIT = 56 * 1024 * 1024


def _cparams(sem):
    return pltpu.CompilerParams(dimension_semantics=sem, vmem_limit_bytes=VMEM_LIMIT)


def _const_spec(shape):
    nd = len(shape)
    return pl.BlockSpec(shape, lambda *_: (0,) * nd, pipeline_mode=pl.Buffered(1))


def _full_spec(shape):
    nd = len(shape)
    return pl.BlockSpec(shape, lambda *_: (0,) * nd)


def _bdot(a, b):
    return jnp.dot(a, b, preferred_element_type=F32)


def _dot_nt(a, b):
    return lax.dot_general(a, b, (((1,), (1,)), ((), ())), preferred_element_type=F32)


def _dot_tn(a, b, precision=None):
    return lax.dot_general(a, b, (((0,), (0,)), ((), ())), preferred_element_type=F32, precision=precision)


def _hdot(a, b):
    return jnp.dot(a, b, preferred_element_type=F32, precision=HIGHEST)


def _sigmoid(x):
    return 1.0 / (1.0 + jnp.exp(-x))


def _silu(x):
    return x * _sigmoid(x)


def _softplus(x):
    return jnp.maximum(x, 0.0) + jnp.log(1.0 + jnp.exp(-jnp.abs(x)))


def _rms(x, w):
    return x * lax.rsqrt(jnp.mean(x * x, axis=-1, keepdims=True) + EPS) * w


def _split_bf16(x):
    hi = x.astype(BF16)
    lo = (x - hi.astype(F32)).astype(BF16)
    return hi, lo


def _inproj_kernel(h_ref, nw_ref, w_ref, q_ref, kb_ref, vb_ref, k_ref, v_ref, xbc_ref, dt_ref, xn_ref):
    xn = _rms(h_ref[...], nw_ref[...]).astype(BF16)
    xn_ref[...] = xn
    qkv = _bdot(xn, w_ref[:, 0:3 * SB_WIDTH])
    q_ref[...] = (qkv[:, 0:SB_WIDTH] * SB_SCALE).astype(BF16)
    k = qkv[:, SB_WIDTH:2 * SB_WIDTH]
    v = qkv[:, 2 * SB_WIDTH:3 * SB_WIDTH]
    k_ref[...] = k
    v_ref[...] = v
    kb_ref[...] = k.astype(BF16)
    vb_ref[...] = v.astype(BF16)
    o = 3 * SB_WIDTH
    xbc_ref[...] = _bdot(xn, w_ref[:, o:o + CONV_DIM])
    dt_ref[...] = _bdot(xn, w_ref[:, o + CONV_DIM:o + CONV_DIM + DT_PAD])


def _inproj(h, norm_w, w_a, tm):
    m, d = h.shape
    wa = w_a.shape[1]
    row = lambda width: pl.BlockSpec((tm, width), lambda i: (i, 0))
    out_shape = (
        jax.ShapeDtypeStruct((m, SB_WIDTH), BF16),
        jax.ShapeDtypeStruct((m, SB_WIDTH), BF16),
        jax.ShapeDtypeStruct((m, SB_WIDTH), BF16),
        jax.ShapeDtypeStruct((m, SB_WIDTH), F32),
        jax.ShapeDtypeStruct((m, SB_WIDTH), F32),
        jax.ShapeDtypeStruct((m, CONV_DIM), F32),
        jax.ShapeDtypeStruct((m, DT_PAD), F32),
        jax.ShapeDtypeStruct((m, d), BF16),
    )
    return pl.pallas_call(
        _inproj_kernel,
        grid=(m // tm,),
        in_specs=[row(d), _const_spec((1, d)), _const_spec((d, wa))],
        out_specs=(row(SB_WIDTH), row(SB_WIDTH), row(SB_WIDTH), row(SB_WIDTH), row(SB_WIDTH),
                   row(CONV_DIM), row(DT_PAD), row(d)),
        out_shape=out_shape,
        compiler_params=_cparams(("parallel",)),
        name="inproj",
    )(h, norm_w, w_a)


def _cumsum_matrix():
    r = lax.broadcasted_iota(jnp.int32, (2 * BLOCK, 2 * BLOCK), 0) % BLOCK
    c = lax.broadcasted_iota(jnp.int32, (2 * BLOCK, 2 * BLOCK), 1)
    return jnp.where((r > c) | (c >= BLOCK), 1.0, 0.0).astype(BF16)


def _attn_kernel(bias_ref, q_ref, k_ref, v_ref, o_ref, acc_ref, carry_ref):
    hg = pl.program_id(1)
    qi = pl.program_id(2)
    lane = lax.broadcasted_iota(jnp.int32, (BLOCK, BLOCK), 1)
    uu = _cumsum_matrix()
    q_heads, biases = [], []
    for h in range(ATTN_HEADS):
        q = q_ref[:, (h // 2) * LANES:(h // 2 + 1) * LANES]
        keep = (lane < SB_HEAD_DIM) if h % 2 == 0 else (lane >= SB_HEAD_DIM)
        q_heads.append(jnp.where(keep, q, jnp.zeros_like(q)))
        biases.append(bias_ref[hg * ATTN_HEADS + h])

    def process(blk, nblk, masked):
        width = nblk * BLOCK
        start = blk * BLOCK if isinstance(blk, int) else pl.multiple_of(blk * BLOCK, BLOCK)
        if masked:
            t_pos = qi * BLOCK + lax.broadcasted_iota(jnp.int32, (BLOCK, width), 0)
            s_pos = start + lax.broadcasted_iota(jnp.int32, (BLOCK, width), 1)
            mask = (s_pos < t_pos) & (s_pos >= PAD)
        for h in range(ATTN_HEADS):
            cols = slice((h // 2) * LANES, (h // 2 + 1) * LANES)
            z = _dot_nt(q_heads[h], k_ref[pl.ds(start, width), cols]) + biases[h]
            lk = -_softplus(z)
            if masked:
                lk = jnp.where(mask, lk, 0.0)
            hi, lo = _split_bf16(lk)
            carry = carry_ref[h]
            between = [None] * nblk
            for j in reversed(range(nblk)):
                sl = slice(j * BLOCK, (j + 1) * BLOCK)
                r = _bdot(jnp.concatenate([hi[:, sl], lo[:, sl]], axis=1), uu)
                between[j] = r[:, :BLOCK] + carry
                carry = carry + r[:, BLOCK:]
            carry_ref[h] = carry
            w = jnp.exp(z + lk + jnp.concatenate(between, axis=1))
            if masked:
                w = jnp.where(mask, w, 0.0)
            acc_ref[h] += _bdot(w.astype(BF16), v_ref[pl.ds(start, width), cols])

    acc_ref[...] = jnp.zeros_like(acc_ref)
    carry_ref[...] = jnp.zeros_like(carry_ref)
    small = qi < ATTN_UNROLL
    pl.when(jnp.logical_not(small))(functools.partial(process, qi, 1, True))
    n_inner = jnp.where(small, 0, qi - 1)
    n_full = n_inner // ATTN_UNROLL

    def body(i, _):
        process(qi - ATTN_UNROLL * (i + 1), ATTN_UNROLL, False)
        return 0

    lax.fori_loop(0, n_full, body, 0)
    last = jnp.where(small, qi, n_inner - ATTN_UNROLL * n_full)
    for r in range(ATTN_UNROLL):
        pl.when(last == r)(functools.partial(process, 0, r + 1, True))

    for p in range(ATTN_HEADS // 2):
        o_ref[:, p * LANES:(p + 1) * LANES] = jnp.where(
            lane < SB_HEAD_DIM, acc_ref[2 * p], acc_ref[2 * p + 1]).astype(o_ref.dtype)


def _sb_attn_prompt(q, k, v, bias):
    b, n, _ = q.shape
    nb = n // BLOCK
    gw = ATTN_HEADS * SB_HEAD_DIM
    return pl.pallas_call(
        _attn_kernel,
        grid=(b, SB_WIDTH // gw, nb),
        in_specs=[
            pl.BlockSpec(memory_space=pltpu.SMEM),
            pl.BlockSpec((None, BLOCK, gw), lambda bi, p, qi: (bi, qi, p)),
            pl.BlockSpec((None, n, gw), lambda bi, p, qi: (bi, 0, p)),
            pl.BlockSpec((None, n, gw), lambda bi, p, qi: (bi, 0, p)),
        ],
        out_specs=pl.BlockSpec((None, BLOCK, gw), lambda bi, p, qi: (bi, qi, p)),
        scratch_shapes=[pltpu.VMEM((ATTN_HEADS, BLOCK, BLOCK), F32), pltpu.VMEM((ATTN_HEADS, BLOCK, BLOCK), F32)],
        out_shape=jax.ShapeDtypeStruct((b, n, SB_WIDTH), BF16),
        compiler_params=_cparams(("parallel", "parallel", "arbitrary")),
        name="sb_attn_prompt",
    )(bias, q, k, v)


def _shift_rows(x, prev_tail, k):
    rolled = pltpu.roll(x, k, axis=0)
    fix = pltpu.roll(prev_tail, k, axis=0)
    row = lax.broadcasted_iota(jnp.int32, prev_tail.shape, 0)
    head = jnp.where(row < k, fix, rolled[:SUBLANES])
    return jnp.concatenate([head, rolled[SUBLANES:]], axis=0)


def _expand_heads(x, e3):
    lane = lax.broadcasted_iota(jnp.int32, x.shape, 1)
    x = jnp.where(lane < SSM_HEADS, x, 0.0)
    hi = x.astype(BF16).astype(F32)
    r1 = x - hi
    mid = r1.astype(BF16).astype(F32)
    lo = r1 - mid
    packed = hi + pltpu.roll(mid, SSM_HEADS, axis=1) + pltpu.roll(lo, 2 * SSM_HEADS, axis=1)
    return _bdot(packed.astype(BF16), e3)


def _ssd_kernel(xbc_ref, dt_ref, cw_ref, cb_ref, dtb_ref, aneg_ref, dskip_ref, e3_ref,
                y_ref, st_ref, cs_ref, hist_ref):
    c = pl.program_id(1)
    row = lax.broadcasted_iota(jnp.int32, (BLOCK, 1), 0)
    valid = jnp.where((c > 0) | (row >= PAD), 1.0, 0.0)

    @pl.when(c == 0)
    def _():
        hist_ref[...] = jnp.zeros_like(hist_ref)
        st_ref[...] = jnp.zeros_like(st_ref)

    x = xbc_ref[...] * valid
    prev_tail = hist_ref[...]
    conv = cb_ref[...] + x * cw_ref[SSM_CONV - 1:SSM_CONV, :]
    for j in range(SSM_CONV - 1):
        conv = conv + _shift_rows(x, prev_tail, SSM_CONV - 1 - j) * cw_ref[j:j + 1, :]
    tail = x[BLOCK - SUBLANES:, :]
    hist_ref[...] = tail
    cs_ref[...] = tail
    xc = _silu(conv)
    xs = xc[:, :SSM_INNER]
    bm = xc[:, SSM_INNER:SSM_INNER + SSM_GROUPS * SSM_STATE].astype(BF16)
    cm = xc[:, SSM_INNER + SSM_GROUPS * SSM_STATE:].astype(BF16)

    dt = _softplus(dt_ref[...] + dtb_ref[...]) * valid
    la = dt * aneg_ref[...]
    ri = lax.broadcasted_iota(jnp.int32, (BLOCK, BLOCK), 0)
    ci = lax.broadcasted_iota(jnp.int32, (BLOCK, BLOCK), 1)
    causal = ci <= ri
    tri = jnp.where(causal, 1.0, 0.0)
    cum = _hdot(tri, la)
    cum_t = cum.T
    e3 = e3_ref[...]
    xdt = xs * _expand_heads(dt, e3)
    ecum = _expand_heads(jnp.exp(cum), e3)
    cum_last = cum[BLOCK - 1:BLOCK, :]
    xdtw = (xdt * _expand_heads(jnp.exp(cum_last - cum), e3)).astype(BF16)
    xdt_b = xdt.astype(BF16)
    ecl = jnp.exp(cum_last)
    lane = ci

    for g in range(SSM_GROUPS):
        cmg = cm[:, g * SSM_STATE:(g + 1) * SSM_STATE]
        bmg = bm[:, g * SSM_STATE:(g + 1) * SSM_STATE]
        cb = _dot_nt(cmg, bmg)
        for pp in range(2):
            pair = 2 * g + pp
            cols = slice(pair * LANES, (pair + 1) * LANES)
            ys = []
            for hh in range(2):
                h = 2 * pair + hh
                seg = cum[:, h:h + 1] - cum_t[h:h + 1, :]
                m = (cb * jnp.where(causal, jnp.exp(seg), 0.0)).astype(BF16)
                ys.append(_bdot(m, xdt_b[:, cols]))
            y = jnp.where(lane < SSM_HEAD_DIM, ys[0], ys[1])
            st = st_ref[cols, :]
            y = y + _dot_nt(cmg, st.astype(BF16)) * ecum[:, cols]
            fac = jnp.where(ri < SSM_HEAD_DIM,
                            jnp.broadcast_to(ecl[:, 2 * pair:2 * pair + 1], (LANES, SSM_STATE)),
                            jnp.broadcast_to(ecl[:, 2 * pair + 1:2 * pair + 2], (LANES, SSM_STATE)))
            st_ref[cols, :] = st * fac + _dot_tn(xdtw[:, cols], bmg)
            y_ref[:, cols] = y + xs[:, cols] * dskip_ref[:, cols]


def _ssd_prompt(xbc, dt, b, cw, cb, dtb, aneg, dskip, e3):
    m = xbc.shape[0]
    nc = m // b // BLOCK
    row = lambda width: pl.BlockSpec((BLOCK, width), lambda bi, c: (bi * nc + c, 0))
    out_shape = (
        jax.ShapeDtypeStruct((m, SSM_INNER), F32),
        jax.ShapeDtypeStruct((b, SSM_INNER, SSM_STATE), F32),
        jax.ShapeDtypeStruct((b, SUBLANES, CONV_DIM), F32),
    )
    return pl.pallas_call(
        _ssd_kernel,
        grid=(b, nc),
        in_specs=[row(CONV_DIM), row(DT_PAD), _const_spec(cw.shape), _const_spec(cb.shape),
                  _const_spec(dtb.shape), _const_spec(aneg.shape), _const_spec(dskip.shape),
                  _const_spec(e3.shape)],
        out_specs=(row(SSM_INNER),
                   pl.BlockSpec((None, SSM_INNER, SSM_STATE), lambda bi, c: (bi, 0, 0)),
                   pl.BlockSpec((None, SUBLANES, CONV_DIM), lambda bi, c: (bi, 0, 0))),
        out_shape=out_shape,
        scratch_shapes=[pltpu.VMEM((SUBLANES, CONV_DIM), F32)],
        compiler_params=_cparams(("parallel", "arbitrary")),
        name="ssd_prompt",
    )(xbc, dt, cw, cb, dtb, aneg, dskip, e3)


def _mix_kernel(xn_ref, h_ref, osb_ref, y_ref, wz_ref, wg_ref, wsb_ref, wssm_ref, wo_ref, snw_ref, n2w_ref,
                h1_ref, xn2_ref):
    xn = xn_ref[...]
    d = h_ref.shape[1]
    z = _bdot(xn, wz_ref[...])
    yz = y_ref[...] * _silu(z)
    gw = SSM_INNER // SSM_GROUPS
    parts = []
    for g in range(SSM_GROUPS):
        yg = yz[:, g * gw:(g + 1) * gw]
        parts.append(yg * lax.rsqrt(jnp.mean(yg * yg, axis=-1, keepdims=True) + EPS))
    yn = (jnp.concatenate(parts, axis=1) * snw_ref[...]).astype(BF16)
    g_a = _sigmoid(_bdot(xn, wg_ref[:, 0:d]))
    merged = g_a * _bdot(osb_ref[...], wsb_ref[...])
    g_s = _sigmoid(_bdot(xn, wg_ref[:, d:2 * d]))
    merged = merged + g_s * _bdot(yn, wssm_ref[...])
    h1 = h_ref[...] + _bdot(merged.astype(BF16), wo_ref[...])
    h1_ref[...] = h1
    xn2_ref[...] = _rms(h1, n2w_ref[...]).astype(BF16)


def _mix_out(xn, h, osb, y, wz, wg, wsb, wssm, wo, snw, n2w, tm):
    m, d = h.shape
    row = lambda width: pl.BlockSpec((tm, width), lambda i: (i, 0))
    return pl.pallas_call(
        _mix_kernel,
        grid=(m // tm,),
        in_specs=[row(d), row(d), row(SB_WIDTH), row(SSM_INNER)] +
                 [_const_spec(a.shape) for a in (wz, wg, wsb, wssm, wo, snw, n2w)],
        out_specs=(row(d), row(d)),
        out_shape=(jax.ShapeDtypeStruct((m, d), F32), jax.ShapeDtypeStruct((m, d), BF16)),
        compiler_params=_cparams(("parallel",)),
        name="mix_out",
    )(xn, h, osb, y, wz, wg, wsb, wssm, wo, snw, n2w)


def _ffn_tail(hdn, h1, wdn_ref, fw_ref, out_ref, final):
    h2 = h1 + _bdot(hdn.astype(BF16), wdn_ref[...])
    out_ref[...] = _rms(h2, fw_ref[...]) if final else h2


def _ffn_prompt_kernel(xn2_ref, h1_ref, wup_ref, wdn_ref, cw_ref, cb_ref, fw_ref, out_ref, fs_ref, hist_ref,
                       *, final):
    t = pl.program_id(1)
    tm = xn2_ref.shape[0]
    f = cw_ref.shape[1]

    @pl.when(t == 0)
    def _():
        hist_ref[0:SUBLANES, :] = jnp.zeros((SUBLANES, f), F32)

    row = lax.broadcasted_iota(jnp.int32, (tm, 1), 0)
    valid = jnp.where((t > 0) | (row >= PAD), 1.0, 0.0)
    xn2 = xn2_ref[...]
    gate = _bdot(xn2, wup_ref[:, 0:f]) * valid
    hist_ref[SUBLANES:SUBLANES + tm, :] = gate
    conv = cb_ref[...] + gate * cw_ref[FFN_CONV - 1:FFN_CONV, :]
    for j in range(FFN_CONV - 1):
        back = FFN_CONV - 1 - j
        conv = conv + hist_ref[SUBLANES - back:SUBLANES - back + tm, :] * cw_ref[j:j + 1, :]
    tail = gate[tm - SUBLANES:, :]
    hist_ref[0:SUBLANES, :] = tail
    fs_ref[...] = tail
    hdn = _silu(conv) * _bdot(xn2, wup_ref[:, f:2 * f])
    _ffn_tail(hdn, h1_ref[...], wdn_ref, fw_ref, out_ref, final)


def _ffn_prompt(xn2, h1, b, wup, wdn, cw, cb, fw, tm, final):
    m, d = h1.shape
    f = cw.shape[1]
    nt = m // b // tm
    row = lambda width: pl.BlockSpec((tm, width), lambda bi, t: (bi * nt + t, 0))
    return pl.pallas_call(
        functools.partial(_ffn_prompt_kernel, final=final),
        grid=(b, nt),
        in_specs=[row(d), row(d)] + [_const_spec(a.shape) for a in (wup, wdn, cw, cb, fw)],
        out_specs=(row(d), pl.BlockSpec((None, SUBLANES, f), lambda bi, t: (bi, 0, 0))),
        out_shape=(jax.ShapeDtypeStruct((m, d), F32), jax.ShapeDtypeStruct((b, SUBLANES, f), F32)),
        scratch_shapes=[pltpu.VMEM((SUBLANES + tm, f), F32)],
        compiler_params=_cparams(("parallel", "arbitrary")),
        name="ffn_prompt",
    )(xn2, h1, wup, wdn, cw, cb, fw)


def _ffn_sample_kernel(xn2_ref, h1_ref, hist_ref, wup_ref, wdn_ref, cw_ref, cb_ref, fw_ref, out_ref, fs_ref,
                       *, final):
    f = cw_ref.shape[1]
    xn2 = xn2_ref[...]
    gate = _bdot(xn2, wup_ref[:, 0:f])
    conv = cb_ref[...] + gate * cw_ref[FFN_CONV - 1:FFN_CONV, :]
    for j in range(FFN_CONV - 1):
        conv = conv + hist_ref[:, j * f:(j + 1) * f] * cw_ref[j:j + 1, :]
    for j in range(FFN_CONV - 2):
        fs_ref[:, j * f:(j + 1) * f] = hist_ref[:, (j + 1) * f:(j + 2) * f]
    fs_ref[:, (FFN_CONV - 2) * f:(FFN_CONV - 1) * f] = gate
    hdn = _silu(conv) * _bdot(xn2, wup_ref[:, f:2 * f])
    _ffn_tail(hdn, h1_ref[...], wdn_ref, fw_ref, out_ref, final)


def _ffn_sample(xn2, h1, hist, wup, wdn, cw, cb, fw, final):
    m, d = h1.shape
    return pl.pallas_call(
        functools.partial(_ffn_sample_kernel, final=final),
        grid=(1,),
        in_specs=[_const_spec(a.shape) for a in (xn2, h1, hist, wup, wdn, cw, cb, fw)],
        out_specs=(_full_spec((m, d)), _full_spec(hist.shape)),
        out_shape=(jax.ShapeDtypeStruct((m, d), F32), jax.ShapeDtypeStruct(hist.shape, F32)),
        compiler_params=_cparams(("arbitrary",)),
        name="ffn_sample",
    )(xn2, h1, hist, wup, wdn, cw, cb, fw)


def _attn_sample_kernel(pt_ref, q_ref, bias_ref, *refs, n_pages):
    k_refs = refs[:n_pages]
    v_refs = refs[n_pages:2 * n_pages]
    o_ref = refs[2 * n_pages]
    hrow = lax.broadcasted_iota(jnp.int32, (SB_HEADS, SB_WIDTH), 0)
    hcol = lax.broadcasted_iota(jnp.int32, (SB_HEADS, SB_WIDTH), 1) // SB_HEAD_DIM
    head_mask = hrow == hcol
    qm = jnp.where(head_mask, jnp.broadcast_to(q_ref[...], (SB_HEADS, SB_WIDTH)), 0.0).astype(BF16)
    bias = bias_ref[...]
    z = jnp.concatenate([_bdot(qm, k_refs[c][...].astype(BF16)) + bias for c in range(n_pages)], axis=0)
    lk = -_softplus(z)
    nr = n_pages * SB_HEADS
    hi, lo = _split_bf16(lk)
    r = _bdot(jnp.concatenate([hi, lo], axis=1), _cumsum_matrix())
    ra = lax.broadcasted_iota(jnp.int32, (nr, nr), 0)
    rb = lax.broadcasted_iota(jnp.int32, (nr, nr), 1)
    mx = jnp.where((ra % SB_HEADS == rb % SB_HEADS) & (rb // SB_HEADS > ra // SB_HEADS), 1.0, 0.0).astype(BF16)
    thi, tlo = _split_bf16(r[:, BLOCK:])
    lt = _bdot(mx, jnp.concatenate([thi, tlo], axis=1))
    w = jnp.exp(z + lk + r[:, :BLOCK] + lt[:, :BLOCK] + lt[:, BLOCK:])
    o_full = jnp.zeros((SB_HEADS, SB_WIDTH), F32)
    for c in range(n_pages):
        wc = w[c * SB_HEADS:(c + 1) * SB_HEADS, :].astype(BF16)
        o_full = o_full + _dot_nt(wc, v_refs[c][...].astype(BF16))
    o_ref[...] = jnp.sum(jnp.where(head_mask, o_full, 0.0), axis=0, keepdims=True)


def _sb_attn_sample(q, bias, cache_kt, cache_vt, page_table, layer):
    s = q.shape[0]
    n_pages = page_table.shape[1]
    page = cache_kt.shape[3]
    assert page == BLOCK and n_pages * SB_HEADS == BLOCK
    q3 = q.reshape(s, 1, SB_WIDTH)
    bias_b = jnp.broadcast_to(bias[:, None], (SB_HEADS, BLOCK))

    def page_spec(c):
        return pl.BlockSpec((None, None, SB_WIDTH, page), lambda i, pt, c=c: (layer, pt[i * n_pages + c], 0, 0))

    grid_spec = pltpu.PrefetchScalarGridSpec(
        num_scalar_prefetch=1,
        grid=(s,),
        in_specs=[pl.BlockSpec((None, 1, SB_WIDTH), lambda i, pt: (i, 0, 0)),
                  pl.BlockSpec((SB_HEADS, BLOCK), lambda i, pt: (0, 0))] +
                 [page_spec(c) for c in range(n_pages)] * 2,
        out_specs=pl.BlockSpec((None, 1, SB_WIDTH), lambda i, pt: (i, 0, 0)),
    )
    out = pl.pallas_call(
        functools.partial(_attn_sample_kernel, n_pages=n_pages),
        grid_spec=grid_spec,
        out_shape=jax.ShapeDtypeStruct((s, 1, SB_WIDTH), F32),
        compiler_params=_cparams(("arbitrary",)),
        name="sb_attn_sample",
    )(page_table.reshape(-1), q3, bias_b, *([cache_kt] * n_pages), *([cache_vt] * n_pages))
    return out.reshape(s, SB_WIDTH)


def _ssd_sample_kernel(xbc_ref, dt_ref, cs_ref, st_ref, cw_ref, cb_ref, dtb_ref, aneg_ref, dskip_ref, e_ref,
                       y_ref, nst_ref, ncs_ref):
    sb = xbc_ref.shape[0]
    x = xbc_ref[...]
    conv = cb_ref[...] + x * cw_ref[SSM_CONV - 1:SSM_CONV, :]
    for j in range(SSM_CONV - 1):
        conv = conv + cs_ref[:, j * CONV_DIM:(j + 1) * CONV_DIM] * cw_ref[j:j + 1, :]
    for j in range(SSM_CONV - 2):
        ncs_ref[:, j * CONV_DIM:(j + 1) * CONV_DIM] = cs_ref[:, (j + 1) * CONV_DIM:(j + 2) * CONV_DIM]
    ncs_ref[:, (SSM_CONV - 2) * CONV_DIM:(SSM_CONV - 1) * CONV_DIM] = x
    xc = _silu(conv)
    xs = xc[:, :SSM_INNER]
    bm = xc[:, SSM_INNER:SSM_INNER + SSM_GROUPS * SSM_STATE]
    cm = xc[:, SSM_INNER + SSM_GROUPS * SSM_STATE:]
    dt = _softplus(dt_ref[...] + dtb_ref[...])
    e = e_ref[...]
    xdt = xs * _hdot(dt, e)
    dec = _hdot(jnp.exp(dt * aneg_ref[...]), e)
    gw = SSM_INNER // SSM_GROUPS
    cb_parts = []
    for g in range(SSM_GROUPS):
        sl = slice(g * SSM_STATE, (g + 1) * SSM_STATE)
        cbg = jnp.sum(cm[:, sl] * bm[:, sl], axis=-1, keepdims=True)
        cb_parts.append(jnp.broadcast_to(cbg, (sb, gw)))
    y = jnp.concatenate(cb_parts, axis=1) * xdt + xs * dskip_ref[...]

    grow = lax.broadcasted_iota(jnp.int32, (SUBLANES, SSM_INNER), 0)
    gcol = lax.broadcasted_iota(jnp.int32, (SUBLANES, SSM_INNER), 1) // gw
    gmask = grow == gcol
    r8 = lax.broadcasted_iota(jnp.int32, (SUBLANES, SSM_STATE), 0)
    row8 = lax.broadcasted_iota(jnp.int32, (SUBLANES, SSM_INNER), 0)
    rsb = lax.broadcasted_iota(jnp.int32, (sb, SSM_INNER), 0)
    ystate = jnp.zeros((sb, SSM_INNER), F32)
    dec_hi = dec.astype(BF16).astype(F32)
    dec_mid = (dec - dec_hi).astype(BF16).astype(F32)
    dec_lo = dec - dec_hi - dec_mid
    x_hi = xdt.astype(BF16).astype(F32)
    x_lo = (xdt - x_hi).astype(BF16).astype(F32)
    b_hi = bm.astype(BF16).astype(F32)
    b_lo = (bm - b_hi).astype(BF16).astype(F32)
    ones3 = jnp.concatenate([jnp.where(r8 < 3, 1.0, 0.0), jnp.zeros((SUBLANES, SSM_STATE), F32)], axis=0).astype(BF16)
    zeros8 = jnp.zeros((SUBLANES, SSM_INNER), F32)

    def bcast8(v, i):
        return jnp.broadcast_to(v[i:i + 1, :], (SUBLANES, SSM_INNER))

    def group_rows(v, i):
        out = jnp.zeros((SUBLANES, SSM_STATE), F32)
        for g in range(SSM_GROUPS):
            vg = jnp.broadcast_to(v[i:i + 1, g * SSM_STATE:(g + 1) * SSM_STATE], (SUBLANES, SSM_STATE))
            out = jnp.where(r8 == g, vg, out)
        return out

    for i in range(sb):
        st = st_ref[i]
        y8 = _dot_nt(group_rows(cm, i).astype(BF16), st.astype(BF16))
        ys = jnp.sum(jnp.where(gmask, y8, 0.0), axis=0, keepdims=True)
        ystate = jnp.where(rsb == i, jnp.broadcast_to(ys, (sb, SSM_INNER)), ystate)
        d8 = jnp.where(row8 == 0, bcast8(dec_hi, i),
                       jnp.where(row8 == 1, bcast8(dec_mid, i), jnp.where(row8 == 2, bcast8(dec_lo, i), 0.0)))
        fac = _dot_tn(jnp.concatenate([d8, zeros8], axis=0).astype(BF16), ones3)
        xh8 = jnp.where(gmask, bcast8(x_hi, i), 0.0)
        xl8 = jnp.where(gmask, bcast8(x_lo, i), 0.0)
        bh8 = group_rows(b_hi, i)
        x16 = jnp.concatenate([xh8 + pltpu.roll(xh8, SSM_GROUPS, axis=0), xl8], axis=0).astype(BF16)
        b16 = jnp.concatenate([bh8 + pltpu.roll(group_rows(b_lo, i), SSM_GROUPS, axis=0), bh8], axis=0).astype(BF16)
        nst_ref[i] = st * fac + _dot_tn(x16, b16)
    y_ref[...] = y + ystate * dec


def _ssd_sample(xbc, dt, cs, state, layer, cw, cb, dtb, aneg, dskip, e, sb):
    s = xbc.shape[0]
    row = lambda width: pl.BlockSpec((sb, width), lambda i: (i, 0))
    cs_w = cs.shape[1]
    return pl.pallas_call(
        _ssd_sample_kernel,
        grid=(s // sb,),
        in_specs=[row(CONV_DIM), row(DT_PAD), row(cs_w),
                  pl.BlockSpec((None, sb, SSM_INNER, SSM_STATE), lambda i: (layer, i, 0, 0))] +
                 [_const_spec(a.shape) for a in (cw, cb, dtb, aneg, dskip, e)],
        out_specs=(row(SSM_INNER),
                   pl.BlockSpec((sb, SSM_INNER, SSM_STATE), lambda i: (i, 0, 0)),
                   row(cs_w)),
        out_shape=(jax.ShapeDtypeStruct((s, SSM_INNER), F32),
                   jax.ShapeDtypeStruct((s, SSM_INNER, SSM_STATE), F32),
                   jax.ShapeDtypeStruct((s, cs_w), F32)),
        compiler_params=_cparams(("parallel",)),
        name="ssd_sample",
    )(xbc, dt, cs, state, cw, cb, dtb, aneg, dskip, e)


def _pad_lanes(v, width):
    return jnp.pad(v.astype(F32), (0, width - v.shape[0])).reshape(1, width)


def kernel(x_prompt, x_sample, cache_k, cache_v, state_ssm, state_conv, state_ffn_conv, page_table,
           meta_tokens, norm1_w, w_in, sb_logit_bias, conv_w, conv_b, dt_bias, a_log, d_skip, ssm_norm_w,
           w_sb_out, w_ssm_out, w_out, norm2_w, w_ffn_up, ffn_conv_w, ffn_conv_b, w_ffn_down, final_norm_w):
    b, s, d = x_prompt.shape
    depth = w_in.shape[0]
    n = s + BLOCK
    ns = x_sample.shape[0]
    f = ffn_conv_w.shape[2]
    assert x_sample.shape[1] == 1 and n % BLOCK == 0
    tm_p = 512
    tm_f = n // 8
    assert (b * n) % tm_p == 0 and tm_f % SUBLANES == 0

    hp = jnp.concatenate([jnp.zeros((b, PAD, d), F32),
                          jnp.broadcast_to(meta_tokens.astype(F32)[None], (b, N_META, d)),
                          x_prompt], axis=1).reshape(b * n, d)
    hs = x_sample.reshape(ns, d)

    hid = jnp.arange(SSM_INNER, dtype=jnp.int32) // SSM_HEAD_DIM
    e_mat = (jnp.arange(DT_PAD, dtype=jnp.int32)[:, None] == hid[None, :]).astype(F32)
    row_head = jnp.arange(DT_PAD, dtype=jnp.int32)
    e3_mat = ((row_head[:, None] % SSM_HEADS == hid[None, :]) &
              (row_head[:, None] < 3 * SSM_HEADS)).astype(BF16)
    state4 = state_ssm.reshape(depth, ns, SSM_INNER, SSM_STATE)
    pool, page = cache_k.shape[1], cache_k.shape[2]
    cache_kt = jnp.transpose(cache_k, (0, 1, 3, 4, 2)).reshape(depth, pool, SB_WIDTH, page)
    cache_vt = jnp.transpose(cache_v, (0, 1, 3, 4, 2)).reshape(depth, pool, SB_WIDTH, page)

    o_qkv = 3 * SB_WIDTH
    o_z = o_qkv
    o_xbc = o_z + SSM_INNER
    o_dt = o_xbc + CONV_DIM
    o_g = o_dt + SSM_HEADS

    outs = {k: [] for k in ("kp", "vp", "sp", "cp", "fp", "ks", "vs", "ss", "cs", "fs")}
    for l in range(depth):
        final = l == depth - 1
        wl = w_in[l]
        w_a = jnp.concatenate([wl[:, :o_qkv], wl[:, o_xbc:o_dt],
                               jnp.pad(wl[:, o_dt:o_g], ((0, 0), (0, DT_PAD - SSM_HEADS)))], axis=1).astype(BF16)
        wz = wl[:, o_z:o_xbc].astype(BF16)
        wg = wl[:, o_g:].astype(BF16)
        wsb = w_sb_out[l].astype(BF16)
        wssm = w_ssm_out[l].astype(BF16)
        wo = w_out[l].astype(BF16)
        wup = w_ffn_up[l].astype(BF16)
        wdn = w_ffn_down[l].astype(BF16)
        n1w = norm1_w[l].reshape(1, d)
        n2w = norm2_w[l].reshape(1, d)
        fw = final_norm_w.reshape(1, d)
        snw = ssm_norm_w[l].reshape(1, SSM_INNER)
        cw = conv_w[l]
        cb = conv_b[l].reshape(1, CONV_DIM)
        dtb = _pad_lanes(dt_bias[l], DT_PAD)
        aneg = _pad_lanes(-jnp.exp(a_log[l].astype(F32)), DT_PAD)
        dskip = jnp.repeat(d_skip[l].astype(F32), SSM_HEAD_DIM).reshape(1, SSM_INNER)
        fcw = ffn_conv_w[l]
        fcb = ffn_conv_b[l].reshape(1, f)
        bias = sb_logit_bias[l].astype(F32)

        q, kb, vb, k32, v32, xbc, dt, xn = _inproj(hp, n1w, w_a, tm_p)
        osb = _sb_attn_prompt(q.reshape(b, n, SB_WIDTH), kb.reshape(b, n, SB_WIDTH),
                              vb.reshape(b, n, SB_WIDTH), bias).reshape(b * n, SB_WIDTH)
        y, st, cst = _ssd_prompt(xbc, dt, b, cw, cb, dtb, aneg, dskip, e3_mat)
        h1, xn2 = _mix_out(xn, hp, osb, y, wz, wg, wsb, wssm, wo, snw, n2w, tm_p)
        hp, fst = _ffn_prompt(xn2, h1, b, wup, wdn, fcw, fcb, fw, tm_f, final)
        outs["kp"].append(k32.reshape(b, n, SB_HEADS, SB_HEAD_DIM)[:, PAD:])
        outs["vp"].append(v32.reshape(b, n, SB_HEADS, SB_HEAD_DIM)[:, PAD:])
        outs["sp"].append(st.reshape(b, SSM_HEADS, SSM_HEAD_DIM, SSM_STATE))
        outs["cp"].append(cst[:, SUBLANES - (SSM_CONV - 1):])
        outs["fp"].append(fst[:, SUBLANES - (FFN_CONV - 1):])

        q, kb, vb, k32, v32, xbc, dt, xn = _inproj(hs, n1w, w_a, ns)
        osb = _sb_attn_sample(q.astype(F32), bias, cache_kt, cache_vt, page_table, l).astype(BF16)
        cs_in = state_conv[l].reshape(ns, (SSM_CONV - 1) * CONV_DIM)
        y, nst, ncs = _ssd_sample(xbc, dt, cs_in, state4, l, cw, cb, dtb, aneg, dskip, e_mat, SUBLANES)
        h1, xn2 = _mix_out(xn, hs, osb, y, wz, wg, wsb, wssm, wo, snw, n2w, ns)
        fs_in = state_ffn_conv[l].reshape(ns, (FFN_CONV - 1) * f)
        hs, nfs = _ffn_sample(xn2, h1, fs_in, wup, wdn, fcw, fcb, fw, final)
        outs["ks"].append(k32.reshape(ns, 1, SB_HEADS, SB_HEAD_DIM))
        outs["vs"].append(v32.reshape(ns, 1, SB_HEADS, SB_HEAD_DIM))
        outs["ss"].append(nst.reshape(ns, SSM_HEADS, SSM_HEAD_DIM, SSM_STATE))
        outs["cs"].append(ncs.reshape(ns, SSM_CONV - 1, CONV_DIM))
        outs["fs"].append(nfs.reshape(ns, FFN_CONV - 1, f))

    y_prompt = hp.reshape(b, n, d)[:, BLOCK:]
    y_sample = hs.reshape(ns, 1, d)
    st = {k: jnp.stack(v) for k, v in outs.items()}
    return (y_prompt, y_sample, st["kp"], st["vp"], st["ks"], st["vs"],
            st["sp"], st["ss"], st["cp"], st["cs"], st["fp"], st["fs"])
```
